```python
import math
import functools
import jax
import jax.numpy as jnp
from jax import lax
import numpy as np

D_MODEL = 1024
BATCH = 4
SEQ = 4096
DEPTH = 1
DEC_BATCH = 128
DEC_SEQ = 4
PAST_LEN = 8192
PAGE_SIZE = 128

A_WIDTH = D_MODEL // 2
A_HEAD_DIM = 64
A_HEADS = A_WIDTH // A_HEAD_DIM
LORA_DECAY = 64
LORA_ICLR = 64
LORA_GATE = 128
N_SHIFT = 3 * A_WIDTH + LORA_DECAY + LORA_ICLR + LORA_GATE
B_WIDTH = D_MODEL // 2
B_HEAD_DIM = 64
B_V_DIM = 2 * B_HEAD_DIM
B_HEADS = B_WIDTH // B_V_DIM
N_COLS = N_SHIFT + 3 * B_WIDTH + 2 * D_MODEL
NUM_BUCKETS = 32
MAX_DISTANCE = 128
Q_BLOCK = 128
FFN_DIM = ((8 * D_MODEL // 3 + 255) // 256) * 256
CONV_WIDTH = 3
LN_EPS = 1e-5
GN_EPS = 64e-5
RMS_EPS = 1e-5
DEEPNORM_ALPHA = (2 * DEPTH) ** 0.25
DEEPNORM_BETA = (8 * DEPTH) ** -0.25

kernel_name = 'hybrid_rwkv7_diffattn_convffn_step'


def layer_norm(x, w, b):
    xf = x.astype(jnp.float32)
    mu = jnp.mean(xf, -1, keepdims=True)
    var = jnp.mean(jnp.square(xf - mu), -1, keepdims=True)
    return ((xf - mu) * lax.rsqrt(var + LN_EPS) * w + b).astype(x.dtype)


def rel_bias(q_pos, k_pos, table):
    n = jnp.maximum(q_pos[:, None] - k_pos[None, :], 0)
    max_exact = NUM_BUCKETS // 2
    nf = jnp.maximum(n, 1).astype(jnp.float32)
    large = max_exact + (jnp.log(nf / max_exact) / math.log(MAX_DISTANCE / max_exact)
                         * (NUM_BUCKETS - max_exact)).astype(jnp.int32)
    bucket = jnp.where(n < max_exact, n, jnp.minimum(large, NUM_BUCKETS - 1))
    return jnp.moveaxis(table[bucket].astype(jnp.float32), -1, 0)


def diff_attend(q, k, v, bias, valid, lam):
    s = jnp.einsum('bqhcd,bkhcd->bhcqk', q.astype(jnp.float32), k.astype(jnp.float32)) * (B_HEAD_DIM ** -0.5)
    s = jnp.where(valid, s + bias[None, :, None], -jnp.inf)
    p = jax.nn.softmax(s, axis=-1)
    a = p[:, :, 0] - lam * p[:, :, 1]
    return jnp.einsum('bhqk,bkhe->bqhe', a, v.astype(jnp.float32))


def attend_prompt(q, k, v, lam, table):
    bsz, seq = q.shape[:2]
    n_blocks = seq // Q_BLOCK
    q_blocks = jnp.swapaxes(q.reshape(bsz, n_blocks, Q_BLOCK, B_HEADS, 2, B_HEAD_DIM), 0, 1)
    k_pos = jnp.arange(seq)

    def block(args):
        i, q_i = args
        q_pos = i * Q_BLOCK + jnp.arange(Q_BLOCK)
        return diff_attend(q_i, k, v, rel_bias(q_pos, k_pos, table), k_pos[None, :] <= q_pos[:, None], lam)

    o = lax.map(block, (jnp.arange(n_blocks), q_blocks))
    return jnp.swapaxes(o, 0, 1).reshape(bsz, seq, B_HEADS, B_V_DIM)


def attend_sample(q, k, v, lam, table, cache_k_l, cache_v_l, page_table):
    n_new = q.shape[1]
    q_pos = PAST_LEN + jnp.arange(n_new)
    k_pos = jnp.arange(PAST_LEN + n_new)
    bias = rel_bias(q_pos, k_pos, table)
    valid = k_pos[None, :] <= q_pos[:, None]

    def one(args):
        pages, q_s, k_s, v_s = args
        k_all = jnp.concatenate([cache_k_l[pages].reshape(-1, B_HEADS, 2, B_HEAD_DIM), k_s.astype(cache_k_l.dtype)], 0)
        v_all = jnp.concatenate([cache_v_l[pages].reshape(-1, B_HEADS, B_V_DIM), v_s.astype(cache_v_l.dtype)], 0)
        return diff_attend(q_s[None], k_all[None], v_all[None], bias, valid, lam)[0]

    return lax.map(one, (page_table, q, k, v))


def rwkv7_time_mix(cols, shift_prev, wkv0, mu_shift, decay_base, decay_up, iclr_base, iclr_up, gate_up,
                   kk_scale, k_iclr_mix, bonus_rk, gn_w, gn_b):
    bsz, seq, _ = cols.shape
    prev = jnp.concatenate([shift_prev[:, None].astype(cols.dtype), cols[:, :-1]], axis=1)
    h = (cols + (prev - cols) * mu_shift).astype(jnp.float32)
    r, k, v, h_w, h_a, h_g = jnp.split(h, [A_WIDTH, 2 * A_WIDTH, 3 * A_WIDTH, 3 * A_WIDTH + LORA_DECAY,
                                           3 * A_WIDTH + LORA_DECAY + LORA_ICLR], axis=-1)
    w_raw = -jax.nn.softplus(-(decay_base + jnp.tanh(h_w) @ decay_up)) - 0.5
    decay = jnp.exp(-jnp.exp(w_raw))
    a = jax.nn.sigmoid(iclr_base + h_a @ iclr_up)
    g = jax.nn.sigmoid(h_g) @ gate_up
    heads = lambda t: t.reshape(bsz, seq, A_HEADS, A_HEAD_DIM)
    kk = heads(k * kk_scale)
    kk = kk / jnp.maximum(jnp.sqrt(jnp.sum(kk * kk, -1, keepdims=True)), 1e-12)
    k = k * (1.0 + (a - 1.0) * k_iclr_mix)
    r, k, v, a, decay = heads(r), heads(k), heads(v), heads(a), heads(decay)

    def step(S, inp):
        r_t, w_t, k_t, v_t, kk_t, a_t = inp
        s_kk = jnp.einsum('bhij,bhj->bhi', S, kk_t)
        S = S * w_t[:, :, None, :] - s_kk[..., None] * (kk_t * a_t)[:, :, None, :] + v_t[..., None] * k_t[:, :, None, :]
        return S, jnp.einsum('bhij,bhj->bhi', S, r_t)

    xs = tuple(jnp.swapaxes(t, 0, 1) for t in (r, decay, k, v, kk, a))
    wkv_final, y = lax.scan(step, wkv0.astype(jnp.float32), xs)
    y = jnp.swapaxes(y, 0, 1)
    y = y + jnp.sum(r * k * bonus_rk, -1, keepdims=True) * v
    mu = jnp.mean(y, -1, keepdims=True)
    var = jnp.mean(jnp.square(y - mu), -1, keepdims=True)
    y = ((y - mu) * lax.rsqrt(var + GN_EPS)).reshape(bsz, seq, A_WIDTH) * gn_w + gn_b
    return y * g, wkv_final, cols[:, -1]


def conv_ffn(x, conv_buf, w_up, conv_w, conv_b, w_down):
    seq = x.shape[1]
    h = jnp.einsum('btd,df->btf', x, w_up)
    hp = jnp.concatenate([conv_buf.astype(h.dtype), h], 1)
    h = conv_b + sum(hp[:, j:j + seq] * conv_w[j] for j in range(CONV_WIDTH))
    u, g = jnp.split(h, 2, axis=-1)
    f = jnp.einsum('btf,fd->btd', jax.nn.gelu(g, approximate=False) * u, w_down)
    return f, hp[:, seq:]


def hybrid_layer(x, shift_prev, wkv0, conv_buf, attend_fn, lam_init,
                 w_in, mu_shift, decay_base, decay_up, iclr_base, iclr_up, gate_up, kk_scale, k_iclr_mix,
                 bonus_rk, gn_w, gn_b, lam_q1, lam_k1, lam_q2, lam_k2, subln_w, w_branch_a, w_branch_b,
                 w_out, ln1_w, ln1_b, w_up, conv_w, conv_b, w_down, ln2_w, ln2_b):
    bsz, seq, _ = x.shape
    p = jnp.einsum('btd,dn->btn', x, w_in)
    o = N_SHIFT
    q = p[..., o:o + B_WIDTH].reshape(bsz, seq, B_HEADS, 2, B_HEAD_DIM)
    k = p[..., o + B_WIDTH:o + 2 * B_WIDTH].reshape(bsz, seq, B_HEADS, 2, B_HEAD_DIM)
    v = p[..., o + 2 * B_WIDTH:o + 3 * B_WIDTH].reshape(bsz, seq, B_HEADS, B_V_DIM)
    o = o + 3 * B_WIDTH
    gate_a = jax.nn.sigmoid(p[..., o:o + D_MODEL].astype(jnp.float32))
    gate_b = jax.nn.sigmoid(p[..., o + D_MODEL:].astype(jnp.float32))
    y_a, wkv_new, shift_new = rwkv7_time_mix(p[..., :N_SHIFT], shift_prev, wkv0, mu_shift, decay_base, decay_up,
                                             iclr_base, iclr_up, gate_up, kk_scale, k_iclr_mix, bonus_rk, gn_w, gn_b)
    lam = (jnp.exp(jnp.sum(lam_q1.astype(jnp.float32) * lam_k1.astype(jnp.float32)))
           - jnp.exp(jnp.sum(lam_q2.astype(jnp.float32) * lam_k2.astype(jnp.float32))) + lam_init)
    o_b = attend_fn(q, k, v, lam)
    o_b = o_b * lax.rsqrt(jnp.mean(jnp.square(o_b), -1, keepdims=True) + RMS_EPS) * subln_w * (1.0 - lam_init)
    o_b = o_b.reshape(bsz, seq, B_WIDTH).astype(x.dtype)
    merged = (gate_a * jnp.einsum('btc,cd->btd', y_a.astype(x.dtype), w_branch_a)
              + gate_b * jnp.einsum('btc,cd->btd', o_b, w_branch_b))
    mix = jnp.einsum('btd,de->bte', merged.astype(x.dtype), w_out)
    x = layer_norm(DEEPNORM_ALPHA * x + mix, ln1_w, ln1_b)
    f, conv_new = conv_ffn(x, conv_buf, w_up, conv_w, conv_b, w_down)
    x = layer_norm(DEEPNORM_ALPHA * x + f, ln2_w, ln2_b)
    return x, k, v, wkv_new, shift_new, conv_new


def setup_inputs(seed: int = 0) -> dict:
    key = jax.random.key(seed)
    ks = iter(jax.random.split(key, 48))

    def nrm(shape, scale):
        return scale * jax.random.normal(next(ks), shape, jnp.float32)

    def unif(shape, lo, hi):
        return jax.random.uniform(next(ks), shape, jnp.float32, lo, hi)

    n_pages = PAST_LEN // PAGE_SIZE
    n_used = DEC_BATCH * n_pages
    n_phys = n_used + max(1, n_used // 4)
    L = DEPTH
    return {
        'x_prompt': nrm((BATCH, SEQ, D_MODEL), 1.0),
        'x_sample': nrm((DEC_BATCH, DEC_SEQ, D_MODEL), 1.0),
        'cache_k': nrm((L, n_phys, PAGE_SIZE, B_HEADS, 2, B_HEAD_DIM), 1.0),
        'cache_v': nrm((L, n_phys, PAGE_SIZE, B_HEADS, B_V_DIM), 1.0),
        'state_wkv': nrm((L, DEC_BATCH, A_HEADS, A_HEAD_DIM, A_HEAD_DIM), 0.5),
        'state_shift': nrm((L, DEC_BATCH, N_SHIFT), 1.0),
        'state_conv': nrm((L, DEC_BATCH, CONV_WIDTH - 1, 2 * FFN_DIM), 1.0),
        'page_table': jax.random.permutation(next(ks), n_phys)[:n_used].reshape(DEC_BATCH, n_pages).astype(jnp.int32),
        'rel_bias_table': nrm((NUM_BUCKETS, B_HEADS), 0.5),
        'w_in': nrm((L, D_MODEL, N_COLS), D_MODEL ** -0.5),
        'mu_shift': unif((L, N_SHIFT), 0.0, 1.0),
        'decay_base': unif((L, A_WIDTH), -6.0, -1.0),
        'decay_up': nrm((L, LORA_DECAY, A_WIDTH), 0.1),
        'iclr_base': nrm((L, A_WIDTH), 0.1),
        'iclr_up': nrm((L, LORA_ICLR, A_WIDTH), 0.1),
        'gate_up': nrm((L, LORA_GATE, A_WIDTH), LORA_GATE ** -0.5),
        'kk_scale': 0.85 + nrm((L, A_WIDTH), 0.02),
        'k_iclr_mix': 1.0 + nrm((L, A_WIDTH), 0.02),
        'bonus_rk': nrm((L, A_HEADS, A_HEAD_DIM), 0.1),
        'gn_w': 1.0 + nrm((L, A_WIDTH), 0.02),
        'gn_b': nrm((L, A_WIDTH), 0.02),
        'lam_q1': nrm((L, B_HEAD_DIM), 0.1),
        'lam_k1': nrm((L, B_HEAD_DIM), 0.1),
        'lam_q2': nrm((L, B_HEAD_DIM), 0.1),
        'lam_k2': nrm((L, B_HEAD_DIM), 0.1),
        'subln_w': 1.0 + nrm((L, B_V_DIM), 0.02),
        'w_branch_a': nrm((L, A_WIDTH, D_MODEL), A_WIDTH ** -0.5),
        'w_branch_b': nrm((L, B_WIDTH, D_MODEL), B_WIDTH ** -0.5),
        'w_out': nrm((L, D_MODEL, D_MODEL), DEEPNORM_BETA * D_MODEL ** -0.5),
        'ln1_w': 1.0 + nrm((L, D_MODEL), 0.02),
        'ln1_b': nrm((L, D_MODEL), 0.02),
        'w_up': nrm((L, D_MODEL, 2 * FFN_DIM), D_MODEL ** -0.5),
        'conv_w': nrm((L, CONV_WIDTH, 2 * FFN_DIM), CONV_WIDTH ** -0.5),
        'conv_b': nrm((L, 2 * FFN_DIM), 0.02),
        'w_down': nrm((L, FFN_DIM, D_MODEL), DEEPNORM_BETA * FFN_DIM ** -0.5),
        'ln2_w': 1.0 + nrm((L, D_MODEL), 0.02),
        'ln2_b': nrm((L, D_MODEL), 0.02),
    }


def reference(x_prompt, x_sample, cache_k, cache_v, state_wkv, state_shift, state_conv, page_table,
              rel_bias_table, w_in, mu_shift, decay_base, decay_up, iclr_base, iclr_up, gate_up, kk_scale,
              k_iclr_mix, bonus_rk, gn_w, gn_b, lam_q1, lam_k1, lam_q2, lam_k2, subln_w, w_branch_a,
              w_branch_b, w_out, ln1_w, ln1_b, w_up, conv_w, conv_b, w_down, ln2_w, ln2_b):
    xp, xs = x_prompt, x_sample
    bp = xp.shape[0]
    kp_l, vp_l, wp_l, sp_l, cp_l = [], [], [], [], []
    ks_l, vs_l, ws_l, ss_l, cs_l = [], [], [], [], []
    for l in range(DEPTH):
        lam_init = 0.8 - 0.6 * math.exp(-0.3 * l)
        lp = (w_in[l], mu_shift[l], decay_base[l], decay_up[l], iclr_base[l], iclr_up[l], gate_up[l],
              kk_scale[l], k_iclr_mix[l], bonus_rk[l], gn_w[l], gn_b[l], lam_q1[l], lam_k1[l], lam_q2[l],
              lam_k2[l], subln_w[l], w_branch_a[l], w_branch_b[l], w_out[l], ln1_w[l], ln1_b[l], w_up[l],
              conv_w[l], conv_b[l], w_down[l], ln2_w[l], ln2_b[l])
        prompt_attend = functools.partial(attend_prompt, table=rel_bias_table)
        sample_attend = functools.partial(attend_sample, table=rel_bias_table, cache_k_l=cache_k[l],
                                          cache_v_l=cache_v[l], page_table=page_table)
        xp, k_p, v_p, wkv_p, sh_p, cv_p = hybrid_layer(
            xp, jnp.zeros((bp, N_SHIFT), xp.dtype),
            jnp.zeros((bp, A_HEADS, A_HEAD_DIM, A_HEAD_DIM), jnp.float32),
            jnp.zeros((bp, CONV_WIDTH - 1, 2 * FFN_DIM), xp.dtype), prompt_attend, lam_init, *lp)
        xs, k_s, v_s, wkv_s, sh_s, cv_s = hybrid_layer(
            xs, state_shift[l], state_wkv[l], state_conv[l], sample_attend, lam_init, *lp)
        kp_l.append(k_p); vp_l.append(v_p); wp_l.append(wkv_p); sp_l.append(sh_p); cp_l.append(cv_p)
        ks_l.append(k_s); vs_l.append(v_s); ws_l.append(wkv_s); ss_l.append(sh_s); cs_l.append(cv_s)
    return (xp, xs,
            jnp.stack(kp_l), jnp.stack(vp_l), jnp.stack(wp_l), jnp.stack(sp_l), jnp.stack(cp_l),
            jnp.stack(ks_l), jnp.stack(vs_l), jnp.stack(ws_l), jnp.stack(ss_l), jnp.stack(cs_l))
```

```python
import functools
import math

import jax
import jax.numpy as jnp
from jax import lax
from jax.experimental import pallas as pl
from jax.experimental.pallas import tpu as pltpu

F32 = jnp.float32
BF16 = jnp.bfloat16

D_MODEL = 1024
DEPTH = 1
A_WIDTH = D_MODEL // 2
A_HEAD_DIM = 64
A_HEADS = A_WIDTH // A_HEAD_DIM
LORA_DECAY = 64
LORA_ICLR = 64
LORA_GATE = 128
N_SHIFT = 3 * A_WIDTH + LORA_DECAY + LORA_ICLR + LORA_GATE
B_WIDTH = D_MODEL // 2
B_HEAD_DIM = 64
B_V_DIM = 2 * B_HEAD_DIM
B_HEADS = B_WIDTH // B_V_DIM
N_COLS = N_SHIFT + 3 * B_WIDTH + 2 * D_MODEL
NUM_BUCKETS = 32
MAX_DISTANCE = 128
FFN_DIM = ((8 * D_MODEL // 3 + 255) // 256) * 256
CONV_WIDTH = 3
LN_EPS = 1e-5
GN_EPS = 64e-5
RMS_EPS = 1e-5
DEEPNORM_ALPHA = (2 * DEPTH) ** 0.25

LANES = 128
SUBLANES = 8
VMEM_LIMIT = 56 * 1024 * 1024

NEG_BIG = -1e30
HEAD_PAIRS = A_HEADS // 2

PROJ_TM = 256
ATT_T = 256
PAGES_PER_STEP = 8
SAMPLE_T_PAD = SUBLANES
MERGE_TM = 512
FFN_TM = 512
FFN_TF = FFN_DIM // 2


def _cparams(*sem):
    return pltpu.CompilerParams(dimension_semantics=sem, vmem_limit_bytes=VMEM_LIMIT)


def _full(shape):
    n = len(shape)
    return pl.BlockSpec(shape, lambda *_: (0,) * n)


def _proj_kernel(x_ref, w_ref, cols_ref, q_ref, k_ref, v_ref, gates_ref):
    xb = x_ref[...].astype(BF16)

    def mm(lo, hi):
        return jnp.dot(xb, w_ref[:, lo:hi], preferred_element_type=F32)

    o = N_SHIFT
    cols_ref[...] = mm(0, o)
    q_ref[...] = mm(o, o + B_WIDTH) * (B_HEAD_DIM ** -0.5)
    k_ref[...] = mm(o + B_WIDTH, o + 2 * B_WIDTH)
    v_ref[...] = mm(o + 2 * B_WIDTH, o + 3 * B_WIDTH)
    gates_ref[...] = mm(o + 3 * B_WIDTH, N_COLS)


def _proj(x2d, w_in_bf16):
    m = x2d.shape[0]
    tm = min(PROJ_TM, m)
    widths = (N_SHIFT, B_WIDTH, B_WIDTH, B_WIDTH, 2 * D_MODEL)
    return pl.pallas_call(
        _proj_kernel,
        grid=(m // tm,),
        in_specs=[pl.BlockSpec((tm, D_MODEL), lambda i: (i, 0)), _full((D_MODEL, N_COLS))],
        out_specs=[pl.BlockSpec((tm, w), lambda i: (i, 0)) for w in widths],
        out_shape=[jax.ShapeDtypeStruct((m, w), F32) for w in widths],
        compiler_params=_cparams("arbitrary"),
    )(x2d, w_in_bf16)


def _pair_lo_mask():
    return lax.broadcasted_iota(jnp.int32, (1, LANES), 1) < A_HEAD_DIM


def _head_sum(x):
    lo = _pair_lo_mask()
    out = []
    for p in range(HEAD_PAIRS):
        xp = x[:, p * LANES:(p + 1) * LANES]
        s_lo = jnp.sum(jnp.where(lo, xp, 0.0), axis=-1, keepdims=True)
        s_hi = jnp.sum(jnp.where(lo, 0.0, xp), axis=-1, keepdims=True)
        out.append(jnp.where(lo, s_lo, s_hi))
    return jnp.concatenate(out, axis=-1)


def _softplus(z):
    return jnp.maximum(z, 0.0) + jnp.log1p(jnp.exp(-jnp.abs(z)))


def _sigmoid(z):
    return 1.0 / (1.0 + jnp.exp(-z))


def _timemix_kernel(cols_ref, shift0_ref, wkv0_ref, mu_ref, dbase_ref, dup_ref, ibase_ref, iup_ref, gup_ref,
                    kks_ref, kim_ref, bonus_ref, gnw_ref, gnb_ref,
                    y_ref, wkv_ref,
                    s_scr, carry_scr, r_scr, w_scr, k_scr, kk_scr, ak_scr, g_scr, bon_scr, vpad_scr, vt_scr, yt_scr,
                    *, bb, tc, n_steps):
    c = pl.program_id(1)
    aw = A_WIDTH

    @pl.when(c == 0)
    def _init():
        s_scr[...] = wkv0_ref[...].reshape(bb * HEAD_PAIRS, A_HEAD_DIM, LANES)
        carry_scr[...] = shift0_ref[...]
        vpad_scr[...] = jnp.zeros_like(vpad_scr)

    rows = lax.broadcasted_iota(jnp.int32, (tc, 1), 0)
    for b in range(bb):
        cols = cols_ref[b]
        prev = jnp.where(rows == 0, carry_scr[b], pltpu.roll(cols, 1, axis=0))
        carry_scr[b] = cols[n_steps - 1:n_steps, :]
        h = cols + (prev - cols) * mu_ref[...]
        r, k, v = h[:, 0:aw], h[:, aw:2 * aw], h[:, 2 * aw:3 * aw]
        h_wa = h[:, 3 * aw:3 * aw + LANES]
        h_g = h[:, 3 * aw + LANES:3 * aw + 2 * LANES]
        z = dbase_ref[...] + jnp.dot(jnp.tanh(h_wa).astype(BF16), dup_ref[...], preferred_element_type=F32)
        w_raw = -_softplus(-z) - 0.5
        decay = jnp.exp(-jnp.exp(w_raw))
        a = _sigmoid(ibase_ref[...] + jnp.dot(h_wa.astype(BF16), iup_ref[...], preferred_element_type=F32))
        g = jnp.dot(_sigmoid(h_g).astype(BF16), gup_ref[...], preferred_element_type=F32)
        kk = k * kks_ref[...]
        kk = kk / jnp.maximum(jnp.sqrt(_head_sum(kk * kk)), 1e-12)
        k = k * (1.0 + (a - 1.0) * kim_ref[...])
        for ref, val in ((r_scr, r), (w_scr, decay), (k_scr, k), (kk_scr, kk), (ak_scr, kk * a)):
            for p in range(HEAD_PAIRS):
                ref[b * HEAD_PAIRS + p] = val[:, p * LANES:(p + 1) * LANES]
        g_scr[b] = g
        bon_scr[b] = _head_sum(r * k * bonus_ref[...]) * v
        if tc == LANES:
            vt_scr[b] = v.T
        else:
            vpad_scr[b, 0:tc, :] = v
            vt_scr[b] = vpad_scr[b].T
    yt_scr[...] = jnp.zeros_like(yt_scr)

    lane = lax.broadcasted_iota(jnp.int32, (1, LANES), 1)
    lo = lane < A_HEAD_DIM
    hd = A_HEAD_DIM

    def seg_sum(x):
        s_lo = jnp.sum(jnp.where(lo, x, 0.0), axis=-1, keepdims=True)
        s_hi = jnp.sum(jnp.where(lo, 0.0, x), axis=-1, keepdims=True)
        return s_lo, s_hi

    n_sub = min(SUBLANES, n_steps)

    def block(i, carry):
        base = pl.multiple_of(i * SUBLANES, SUBLANES)
        for b in range(bb):
            for p in range(HEAD_PAIRS):
                idx = b * HEAD_PAIRS + p
                ra, rb = p * LANES, p * LANES + hd
                tiles = [ref[idx, pl.ds(base, SUBLANES), :] for ref in (r_scr, w_scr, k_scr, kk_scr, ak_scr)]
                s = s_scr[idx]
                ya_acc = yt_scr[b, ra:ra + hd, :]
                yb_acc = yt_scr[b, rb:rb + hd, :]
                for j in range(n_sub):
                    tmask = lane == base + j
                    r_t, w_t, k_t, kk_t, ak_t = (jnp.broadcast_to(x[j:j + 1, :], (hd, LANES)) for x in tiles)
                    v_a = jnp.sum(jnp.where(tmask, vt_scr[b, ra:ra + hd, :], 0.0), axis=-1, keepdims=True)
                    v_b = jnp.sum(jnp.where(tmask, vt_scr[b, rb:rb + hd, :], 0.0), axis=-1, keepdims=True)
                    v_col = jnp.where(lo, v_a, v_b)
                    u_a, u_b = seg_sum(s * kk_t)
                    s = s * w_t - jnp.where(lo, u_a, u_b) * ak_t + v_col * k_t
                    y_a, y_b = seg_sum(s * r_t)
                    ya_acc = jnp.where(tmask, y_a, ya_acc)
                    yb_acc = jnp.where(tmask, y_b, yb_acc)
                s_scr[idx] = s
                yt_scr[b, ra:ra + hd, :] = ya_acc
                yt_scr[b, rb:rb + hd, :] = yb_acc
        return carry

    lax.fori_loop(0, n_steps // n_sub, block, 0)

    inv_n = 1.0 / A_HEAD_DIM
    for b in range(bb):
        y = yt_scr[b].T[0:tc, :] + bon_scr[b]
        mu = _head_sum(y) * inv_n
        d = y - mu
        var = _head_sum(d * d) * inv_n
        y = d * lax.rsqrt(var + GN_EPS) * gnw_ref[...] + gnb_ref[...]
        y_ref[b] = y * g_scr[b]

    @pl.when(c == pl.num_programs(1) - 1)
    def _fin():
        wkv_ref[...] = s_scr[...].reshape(bb, HEAD_PAIRS, A_HEAD_DIM, LANES)


def _pack_pairs(wkv):
    b = wkv.shape[0]
    x = wkv.reshape(b, HEAD_PAIRS, 2, A_HEAD_DIM, A_HEAD_DIM)
    return jnp.swapaxes(x, 2, 3).reshape(b, HEAD_PAIRS, A_HEAD_DIM, LANES)


def _unpack_pairs(x):
    b = x.shape[0]
    x = x.reshape(b, HEAD_PAIRS, A_HEAD_DIM, 2, A_HEAD_DIM)
    return jnp.swapaxes(x, 2, 3).reshape(b, A_HEADS, A_HEAD_DIM, A_HEAD_DIM)


def _timemix(cols, shift0, wkv0, tm_params, *, bb, tc, n_steps):
    bsz, seq, _ = cols.shape
    nb, nc = bsz // bb, seq // tc
    vec = lambda a: a.reshape(1, -1)
    mu, dbase, dup, ibase, iup, gup, kks, kim, bonus, gnw, gnb = tm_params
    zpad = jnp.zeros((LORA_DECAY, A_WIDTH), F32)
    dup_p = jnp.concatenate([dup, zpad], 0).astype(BF16)
    iup_p = jnp.concatenate([zpad, iup], 0).astype(BF16)
    params = [vec(mu), vec(dbase), dup_p, vec(ibase), iup_p, gup.astype(BF16), vec(kks), vec(kim), vec(bonus),
              vec(gnw), vec(gnb)]
    scr = lambda *s: pltpu.VMEM(s, F32)
    y, wkv = pl.pallas_call(
        functools.partial(_timemix_kernel, bb=bb, tc=tc, n_steps=n_steps),
        grid=(nb, nc),
        in_specs=[pl.BlockSpec((bb, tc, N_SHIFT), lambda i, c: (i, c, 0)),
                  pl.BlockSpec((bb, 1, N_SHIFT), lambda i, c: (i, 0, 0)),
                  pl.BlockSpec((bb, HEAD_PAIRS, A_HEAD_DIM, LANES), lambda i, c: (i, 0, 0, 0))]
                 + [_full(p.shape) for p in params],
        out_specs=[pl.BlockSpec((bb, tc, A_WIDTH), lambda i, c: (i, c, 0)),
                   pl.BlockSpec((bb, HEAD_PAIRS, A_HEAD_DIM, LANES), lambda i, c: (i, 0, 0, 0))],
        out_shape=[jax.ShapeDtypeStruct((bsz, seq, A_WIDTH), F32),
                   jax.ShapeDtypeStruct((bsz, HEAD_PAIRS, A_HEAD_DIM, LANES), F32)],
        scratch_shapes=[scr(bb * HEAD_PAIRS, A_HEAD_DIM, LANES), scr(bb, 1, N_SHIFT)]
                       + [scr(bb * HEAD_PAIRS, tc, LANES)] * 5 + [scr(bb, tc, A_WIDTH)] * 2
                       + [scr(bb, LANES, A_WIDTH), scr(bb, A_WIDTH, LANES), scr(bb, A_WIDTH, LANES)],
        compiler_params=_cparams("arbitrary", "arbitrary"),
    )(cols, shift0.reshape(bsz, 1, N_SHIFT), _pack_pairs(wkv0), *params)
    return y, _unpack_pairs(wkv)


def _bucket(n):
    max_exact = NUM_BUCKETS // 2
    nf = jnp.maximum(n, 1).astype(F32)
    large = max_exact + (jnp.log(nf / max_exact) / math.log(MAX_DISTANCE / max_exact)
                         * (NUM_BUCKETS - max_exact)).astype(jnp.int32)
    return jnp.where(n < max_exact, n, jnp.minimum(large, NUM_BUCKETS - 1))


def _prompt_bias_kernel(table_ref, out_ref):
    h = pl.program_id(0)
    t = ATT_T
    rr = lax.broadcasted_iota(jnp.int32, (t, t), 0)
    cc = lax.broadcasted_iota(jnp.int32, (t, t), 1)
    for i in range(3):
        d = (2 - i) * t + rr - cc
        bkt = _bucket(jnp.maximum(d, 0))
        bias = jnp.zeros((t, t), F32)
        for bk in range(NUM_BUCKETS):
            bias = jnp.where(bkt == bk, table_ref[bk, h], bias)
        out_ref[0, i] = jnp.where(d >= 0, bias, NEG_BIG)


def _prompt_bias(table):
    t = ATT_T
    return pl.pallas_call(
        _prompt_bias_kernel,
        grid=(B_HEADS,),
        in_specs=[pl.BlockSpec(memory_space=pltpu.SMEM)],
        out_specs=pl.BlockSpec((1, 3, t, t), lambda h: (h, 0, 0, 0)),
        out_shape=jax.ShapeDtypeStruct((B_HEADS, 3, t, t), F32),
        compiler_params=_cparams("arbitrary"),
    )(table)


def _sample_rows(n_new):
    return n_new * 2 * B_HEADS


def _sample_bias_kernel(table_ref, out_ref, *, n_new, page):
    nr = _sample_rows(n_new)
    rr = lax.broadcasted_iota(jnp.int32, (nr, LANES), 0)
    cc = lax.broadcasted_iota(jnp.int32, (nr, LANES), 1)
    tok = rr // (2 * B_HEADS)
    head = (rr % (2 * B_HEADS)) // 2

    def lookup(d):
        bkt = _bucket(jnp.maximum(d, 0))
        bias = jnp.zeros((nr, LANES), F32)
        for hh in range(B_HEADS):
            for bk in range(NUM_BUCKETS):
                bias = jnp.where((bkt == bk) & (head == hh), table_ref[bk, hh], bias)
        return bias

    far = lookup(jnp.full((nr, LANES), MAX_DISTANCE, jnp.int32))
    out_ref[0] = far
    out_ref[1] = lookup(tok + page - cc)
    d_self = tok - cc
    out_ref[2] = jnp.where((d_self >= 0) & (cc < n_new), lookup(d_self), NEG_BIG)


def _sample_bias(table, n_new, page):
    nr = _sample_rows(n_new)
    return pl.pallas_call(
        functools.partial(_sample_bias_kernel, n_new=n_new, page=page),
        in_specs=[pl.BlockSpec(memory_space=pltpu.SMEM)],
        out_specs=pl.BlockSpec(memory_space=pltpu.VMEM),
        out_shape=jax.ShapeDtypeStruct((3, nr, LANES), F32),
    )(table)


def _lambda(lq1_ref, lk1_ref, lq2_ref, lk2_ref, lam_init):
    s1 = jnp.sum(lq1_ref[...] * lk1_ref[...], axis=-1, keepdims=True)
    s2 = jnp.sum(lq2_ref[...] * lk2_ref[...], axis=-1, keepdims=True)
    return jnp.exp(s1) - jnp.exp(s2) + lam_init


def _nt_dot(a, b):
    return lax.dot_general(a, b, (((1,), (1,)), ((), ())), preferred_element_type=F32)


def _attn_prompt_kernel(q_ref, k_ref, v_ref, bias_ref, lq1_ref, lk1_ref, lq2_ref, lk2_ref, subln_ref, o_ref,
                        acc1_scr, acc2_scr, *, lam_init):
    i = pl.program_id(2)
    t = ATT_T
    lane = lax.broadcasted_iota(jnp.int32, (1, LANES), 1)
    first = lane < B_HEAD_DIM
    q = q_ref[0]
    q1 = jnp.where(first, q, 0.0).astype(BF16)
    q2 = jnp.where(first, 0.0, q).astype(BF16)
    acc1_scr[...] = jnp.zeros_like(acc1_scr)
    acc2_scr[...] = jnp.zeros_like(acc2_scr)

    def online(s, m, l, acc_scr, vt):
        m_new = jnp.maximum(m, jnp.max(s, axis=-1, keepdims=True))
        p = jnp.exp(s - m_new)
        alpha = jnp.exp(m - m_new)
        l = alpha * l + jnp.sum(p, axis=-1, keepdims=True)
        acc_scr[...] = alpha * acc_scr[...] + jnp.dot(p.astype(BF16), vt, preferred_element_type=F32)
        return m_new, l

    def tile(j, carry):
        m1, l1, m2, l2 = carry
        off = pl.multiple_of(j * t, t)
        kt = k_ref[0, pl.ds(off, t), :].astype(BF16)
        vt = v_ref[0, pl.ds(off, t), :].astype(BF16)
        bias = bias_ref[0, jnp.clip(j - i + 2, 0, 2)]
        m1, l1 = online(_nt_dot(q1, kt) + bias, m1, l1, acc1_scr, vt)
        m2, l2 = online(_nt_dot(q2, kt) + bias, m2, l2, acc2_scr, vt)
        return m1, l1, m2, l2

    m0 = jnp.full((t, 1), NEG_BIG, F32)
    l0 = jnp.zeros((t, 1), F32)
    _, l1, _, l2 = lax.fori_loop(0, i + 1, tile, (m0, l0, m0, l0))
    lam = _lambda(lq1_ref, lk1_ref, lq2_ref, lk2_ref, lam_init)
    o = acc1_scr[...] * (1.0 / l1) - acc2_scr[...] * (lam / l2)
    ms = jnp.sum(o * o, axis=-1, keepdims=True) * (1.0 / B_V_DIM)
    o_ref[0] = o * lax.rsqrt(ms + RMS_EPS) * subln_ref[...] * (1.0 - lam_init)


def _attn_prompt(q, k, v, bias, lam_params, subln, lam_init):
    bsz, seq, _ = q.shape
    t = ATT_T
    lam_specs = [_full((1, B_HEAD_DIM))] * 4
    return pl.pallas_call(
        functools.partial(_attn_prompt_kernel, lam_init=lam_init),
        grid=(bsz, B_HEADS, seq // t),
        in_specs=[pl.BlockSpec((1, t, B_V_DIM), lambda b, h, i: (b, i, h)),
                  pl.BlockSpec((1, seq, B_V_DIM), lambda b, h, i: (b, 0, h)),
                  pl.BlockSpec((1, seq, B_V_DIM), lambda b, h, i: (b, 0, h)),
                  pl.BlockSpec((1, 3, t, t), lambda b, h, i: (h, 0, 0, 0))]
                 + lam_specs + [_full((1, B_V_DIM))],
        out_specs=pl.BlockSpec((1, t, B_V_DIM), lambda b, h, i: (b, i, h)),
        out_shape=jax.ShapeDtypeStruct((bsz, seq, B_WIDTH), F32),
        scratch_shapes=[pltpu.VMEM((t, B_V_DIM), F32)] * 2,
        compiler_params=_cparams("arbitrary", "arbitrary", "arbitrary"),
    )(q, k, v, bias, *lam_params, subln)


def _attn_sample_kernel(pt_ref, q_ref, kn_ref, vn_ref, sb_ref, lq1_ref, lk1_ref, lq2_ref, lk2_ref, subln_ref,
                        *rest, n_new, lam_init):
    pg = PAGES_PER_STEP
    k_refs, v_refs = rest[:pg], rest[pg:2 * pg]
    o_ref = rest[2 * pg]
    qb_scr, m_scr, l_scr, acc_scr, kn_scr, vn_scr = rest[2 * pg + 1:]
    del pt_ref
    g = pl.program_id(1)
    ng = pl.num_programs(1)
    nr = _sample_rows(n_new)
    hc = 2 * B_HEADS

    @pl.when(g == 0)
    def _init():
        rr = lax.broadcasted_iota(jnp.int32, (hc, B_WIDTH), 0)
        cc = lax.broadcasted_iota(jnp.int32, (hc, B_WIDTH), 1)
        sel = (cc // B_HEAD_DIM) == rr
        q = q_ref[0]
        for tok in range(n_new):
            row = jnp.broadcast_to(q[tok:tok + 1, :], (hc, B_WIDTH))
            qb_scr[tok * hc:(tok + 1) * hc, :] = jnp.where(sel, row, 0.0).astype(BF16)
        m_scr[...] = jnp.full_like(m_scr, NEG_BIG)
        l_scr[...] = jnp.zeros_like(l_scr)
        acc_scr[...] = jnp.zeros_like(acc_scr)
        kn_scr[...] = jnp.zeros_like(kn_scr)
        vn_scr[...] = jnp.zeros_like(vn_scr)
        kn_scr[0:SAMPLE_T_PAD, :] = kn_ref[0]
        vn_scr[0:SAMPLE_T_PAD, :] = vn_ref[0]

    def update(s_list, v_list):
        m = m_scr[...]
        m_new = m
        for s in s_list:
            m_new = jnp.maximum(m_new, jnp.max(s, axis=-1, keepdims=True))
        alpha = jnp.exp(m - m_new)
        l = alpha * l_scr[...]
        acc = alpha * acc_scr[...]
        for s, vv in zip(s_list, v_list):
            p = jnp.exp(s - m_new)
            l = l + jnp.sum(p, axis=-1, keepdims=True)
            acc = acc + jnp.dot(p.astype(BF16), vv, preferred_element_type=F32)
        m_scr[...] = m_new
        l_scr[...] = l
        acc_scr[...] = acc

    qb = qb_scr[...]
    far = sb_ref[0]
    s_list, v_list = [], []
    for i in range(pg):
        bias = far
        if i == pg - 1:
            bias = sb_ref[jnp.where(g == ng - 1, 1, 0)]
        s_list.append(_nt_dot(qb, k_refs[i][0].astype(BF16)) + bias)
        v_list.append(v_refs[i][0].astype(BF16))
    update(s_list, v_list)

    @pl.when(g == ng - 1)
    def _fin():
        update([_nt_dot(qb, kn_scr[...].astype(BF16)) + sb_ref[2]], [vn_scr[...].astype(BF16)])
        lam = _lambda(lq1_ref, lk1_ref, lq2_ref, lk2_ref, lam_init)
        rr = lax.broadcasted_iota(jnp.int32, (nr, 1), 0)
        coef = jnp.where(rr % 2 == 0, 1.0, -lam) / l_scr[...]
        r2 = lax.broadcasted_iota(jnp.int32, (nr, B_WIDTH), 0)
        c2 = lax.broadcasted_iota(jnp.int32, (nr, B_WIDTH), 1)
        own = (c2 // B_V_DIM) == ((r2 % hc) // 2)
        w = jnp.where(own, acc_scr[...] * coef, 0.0)
        orow = lax.broadcasted_iota(jnp.int32, (SAMPLE_T_PAD, 1), 0)
        o = jnp.zeros((SAMPLE_T_PAD, B_WIDTH), F32)
        for tok in range(n_new):
            o = jnp.where(orow == tok, jnp.sum(w[tok * hc:(tok + 1) * hc, :], axis=0, keepdims=True), o)
        parts = []
        for h in range(B_HEADS):
            oh = o[:, h * B_V_DIM:(h + 1) * B_V_DIM]
            ms = jnp.sum(oh * oh, axis=-1, keepdims=True) * (1.0 / B_V_DIM)
            parts.append(oh * lax.rsqrt(ms + RMS_EPS) * subln_ref[...] * (1.0 - lam_init))
        o_ref[0] = jnp.concatenate(parts, axis=-1)


def _attn_sample(q, k_new, v_new, cache_k, cache_v, page_table, sbias, lam_params, subln, lam_init, n_new):
    bsz = q.shape[0]
    n_pages = page_table.shape[1]
    page = cache_k.shape[1]
    pg = PAGES_PER_STEP
    nr = _sample_rows(n_new)
    tok_spec = pl.BlockSpec((1, SAMPLE_T_PAD, B_WIDTH), lambda b, g, pt: (b, 0, 0))

    def page_spec(i):
        return pl.BlockSpec((1, page, B_WIDTH), lambda b, g, pt: (pt[b, g * pg + i], 0, 0))

    const = lambda shape: pl.BlockSpec(shape, lambda b, g, pt: (0,) * len(shape))
    grid_spec = pltpu.PrefetchScalarGridSpec(
        num_scalar_prefetch=1,
        grid=(bsz, n_pages // pg),
        in_specs=[tok_spec, tok_spec, tok_spec, const((3, nr, LANES))]
                 + [const((1, B_HEAD_DIM))] * 4 + [const((1, B_V_DIM))]
                 + [page_spec(i) for i in range(pg)] * 2,
        out_specs=tok_spec,
        scratch_shapes=[pltpu.VMEM((nr, B_WIDTH), BF16), pltpu.VMEM((nr, 1), F32), pltpu.VMEM((nr, 1), F32),
                        pltpu.VMEM((nr, B_WIDTH), F32), pltpu.VMEM((page, B_WIDTH), F32),
                        pltpu.VMEM((page, B_WIDTH), F32)],
    )
    return pl.pallas_call(
        functools.partial(_attn_sample_kernel, n_new=n_new, lam_init=lam_init),
        grid_spec=grid_spec,
        out_shape=jax.ShapeDtypeStruct((bsz, SAMPLE_T_PAD, B_WIDTH), F32),
        compiler_params=_cparams("arbitrary", "arbitrary"),
    )(page_table, q, k_new, v_new, sbias, *lam_params, subln, *([cache_k] * pg), *([cache_v] * pg))


def _layer_norm(z, w, b):
    inv_d = 1.0 / D_MODEL
    mu = jnp.sum(z, axis=-1, keepdims=True) * inv_d
    d = z - mu
    var = jnp.sum(d * d, axis=-1, keepdims=True) * inv_d
    return d * lax.rsqrt(var + LN_EPS) * w + b


def _merge_kernel(x_ref, ya_ref, ob_ref, gates_ref, wa_ref, wb_ref, wo_ref, lnw_ref, lnb_ref, o_ref):
    ga = _sigmoid(gates_ref[:, 0:D_MODEL])
    gb = _sigmoid(gates_ref[:, D_MODEL:2 * D_MODEL])
    merged = (ga * jnp.dot(ya_ref[...].astype(BF16), wa_ref[...], preferred_element_type=F32)
              + gb * jnp.dot(ob_ref[...].astype(BF16), wb_ref[...], preferred_element_type=F32))
    mix = jnp.dot(merged.astype(BF16), wo_ref[...], preferred_element_type=F32)
    o_ref[...] = _layer_norm(DEEPNORM_ALPHA * x_ref[...] + mix, lnw_ref[...], lnb_ref[...])


def _merge(x2d, ya, ob, gates, wa, wb, wo, lnw, lnb):
    m = x2d.shape[0]
    tm = min(MERGE_TM, m)
    row = lambda w: pl.BlockSpec((tm, w), lambda i: (i, 0))
    return pl.pallas_call(
        _merge_kernel,
        grid=(m // tm,),
        in_specs=[row(D_MODEL), row(A_WIDTH), row(B_WIDTH), row(2 * D_MODEL),
                  _full(wa.shape), _full(wb.shape), _full(wo.shape), _full((1, D_MODEL)), _full((1, D_MODEL))],
        out_specs=row(D_MODEL),
        out_shape=jax.ShapeDtypeStruct((m, D_MODEL), F32),
        compiler_params=_cparams("arbitrary"),
    )(x2d, ya, ob, gates, wa, wb, wo, lnw.reshape(1, -1), lnb.reshape(1, -1))


def _gelu(x):
    return 0.5 * x * (1.0 + lax.erf(x * (2.0 ** -0.5)))


def _ffn_prompt_kernel(x_ref, wu_ref, wg_ref, cwu_ref, cwg_ref, cbu_ref, cbg_ref, wd_ref, lnw_ref, lnb_ref,
                       o_ref, tu_ref, tg_ref, acc_scr, cu_scr, cg_scr):
    i, j = pl.program_id(1), pl.program_id(2)
    tm = x_ref.shape[1]
    rows = lax.broadcasted_iota(jnp.int32, (tm, 1), 0)
    xb = x_ref[0].astype(BF16)

    @pl.when(i == 0)
    def _reset():
        cu_scr[j] = jnp.zeros((SUBLANES, FFN_TF), F32)
        cg_scr[j] = jnp.zeros((SUBLANES, FFN_TF), F32)

    def conv(w_ref, cw_ref, cb_ref, carry_scr, tail_ref):
        h = jnp.dot(xb, w_ref[...], preferred_element_type=F32)
        carry = carry_scr[j]
        c6, c7 = carry[SUBLANES - 2:SUBLANES - 1, :], carry[SUBLANES - 1:SUBLANES, :]
        h1 = jnp.where(rows == 0, c7, pltpu.roll(h, 1, axis=0))
        h2 = jnp.where(rows == 0, c6, jnp.where(rows == 1, c7, pltpu.roll(h, 2, axis=0)))
        tail = h[tm - SUBLANES:tm, :]
        carry_scr[j] = tail
        tail_ref[0, 0] = tail
        return cb_ref[...] + (cw_ref[0:1, :] * h2 + cw_ref[1:2, :] * h1 + cw_ref[2:3, :] * h)

    u = conv(wu_ref, cwu_ref, cbu_ref, cu_scr, tu_ref)
    g = conv(wg_ref, cwg_ref, cbg_ref, cg_scr, tg_ref)
    part = jnp.dot((_gelu(g) * u).astype(BF16), wd_ref[...], preferred_element_type=F32)

    @pl.when(j == 0)
    def _first():
        acc_scr[...] = part

    @pl.when(j > 0)
    def _rest():
        acc_scr[...] += part

    @pl.when(j == pl.num_programs(2) - 1)
    def _fin():
        o_ref[0] = _layer_norm(DEEPNORM_ALPHA * x_ref[0] + acc_scr[...], lnw_ref[...], lnb_ref[...])


def _ffn_prompt(x, w_up, conv_w, conv_b, w_down, lnw, lnb):
    bsz, seq, _ = x.shape
    tm, tf = FFN_TM, FFN_TF
    nf = FFN_DIM // tf
    nt = seq // tm
    conv_b = conv_b.reshape(1, -1)
    ucol = lambda r: pl.BlockSpec((r, tf), lambda b, i, j: (0, j))
    gcol = lambda r: pl.BlockSpec((r, tf), lambda b, i, j: (0, nf + j))
    tail = pl.BlockSpec((1, 1, SUBLANES, tf), lambda b, i, j: (b, i, 0, j))
    return pl.pallas_call(
        _ffn_prompt_kernel,
        grid=(bsz, nt, nf),
        in_specs=[pl.BlockSpec((1, tm, D_MODEL), lambda b, i, j: (b, i, 0)),
                  ucol(D_MODEL), gcol(D_MODEL), ucol(CONV_WIDTH), gcol(CONV_WIDTH), ucol(1), gcol(1),
                  pl.BlockSpec((tf, D_MODEL), lambda b, i, j: (j, 0)),
                  pl.BlockSpec((1, D_MODEL), lambda b, i, j: (0, 0)),
                  pl.BlockSpec((1, D_MODEL), lambda b, i, j: (0, 0))],
        out_specs=[pl.BlockSpec((1, tm, D_MODEL), lambda b, i, j: (b, i, 0)), tail, tail],
        out_shape=[jax.ShapeDtypeStruct((bsz, seq, D_MODEL), F32),
                   jax.ShapeDtypeStruct((bsz, nt, SUBLANES, FFN_DIM), F32),
                   jax.ShapeDtypeStruct((bsz, nt, SUBLANES, FFN_DIM), F32)],
        scratch_shapes=[pltpu.VMEM((tm, D_MODEL), F32), pltpu.VMEM((nf, SUBLANES, tf), F32),
                        pltpu.VMEM((nf, SUBLANES, tf), F32)],
        compiler_params=_cparams("arbitrary", "arbitrary", "arbitrary"),
    )(x, w_up, w_up, conv_w, conv_w, conv_b, conv_b, w_down, lnw.reshape(1, -1), lnb.reshape(1, -1))


def _ffn_sample_kernel(x_ref, bu_ref, bg_ref, wu_ref, wg_ref, cwu_ref, cwg_ref, cbu_ref, cbg_ref, wd_ref,
                       lnw_ref, lnb_ref, o_ref, tu_ref, tg_ref, acc_scr, *, n_new, bsz):
    j = pl.program_id(0)
    m = n_new * bsz
    nbuf = (CONV_WIDTH - 1) * bsz
    xb = x_ref[...].astype(BF16)

    def conv(w_ref, cw_ref, cb_ref, buf_ref, tail_ref):
        h = jnp.dot(xb, w_ref[...], preferred_element_type=F32)
        hp = jnp.concatenate([buf_ref[...], h], axis=0)
        tail_ref[...] = hp[m:m + nbuf, :]
        return cb_ref[...] + (cw_ref[0:1, :] * hp[0:m, :] + cw_ref[1:2, :] * hp[bsz:bsz + m, :]
                              + cw_ref[2:3, :] * hp[2 * bsz:2 * bsz + m, :])

    u = conv(wu_ref, cwu_ref, cbu_ref, bu_ref, tu_ref)
    g = conv(wg_ref, cwg_ref, cbg_ref, bg_ref, tg_ref)
    part = jnp.dot((_gelu(g) * u).astype(BF16), wd_ref[...], preferred_element_type=F32)

    @pl.when(j == 0)
    def _first():
        acc_scr[...] = part

    @pl.when(j > 0)
    def _rest():
        acc_scr[...] += part

    @pl.when(j == pl.num_programs(0) - 1)
    def _fin():
        o_ref[...] = _layer_norm(DEEPNORM_ALPHA * x_ref[...] + acc_scr[...], lnw_ref[...], lnb_ref[...])


def _ffn_sample(x_pm, buf_pm, w_up, conv_w, conv_b, w_down, lnw, lnb, n_new, bsz):
    m = n_new * bsz
    nbuf = (CONV_WIDTH - 1) * bsz
    tf = FFN_TF
    nf = FFN_DIM // tf
    conv_b = conv_b.reshape(1, -1)
    ucol = lambda r: pl.BlockSpec((r, tf), lambda j: (0, j))
    gcol = lambda r: pl.BlockSpec((r, tf), lambda j: (0, nf + j))
    return pl.pallas_call(
        functools.partial(_ffn_sample_kernel, n_new=n_new, bsz=bsz),
        grid=(nf,),
        in_specs=[_full((m, D_MODEL)), ucol(nbuf), gcol(nbuf), ucol(D_MODEL), gcol(D_MODEL),
                  ucol(CONV_WIDTH), gcol(CONV_WIDTH), ucol(1), gcol(1),
                  pl.BlockSpec((tf, D_MODEL), lambda j: (j, 0)), _full((1, D_MODEL)), _full((1, D_MODEL))],
        out_specs=[_full((m, D_MODEL)), ucol(nbuf), ucol(nbuf)],
        out_shape=[jax.ShapeDtypeStruct((m, D_MODEL), F32), jax.ShapeDtypeStruct((nbuf, FFN_DIM), F32),
                   jax.ShapeDtypeStruct((nbuf, FFN_DIM), F32)],
        scratch_shapes=[pltpu.VMEM((m, D_MODEL), F32)],
        compiler_params=_cparams("arbitrary"),
    )(x_pm, buf_pm, buf_pm, w_up, w_up, conv_w, conv_w, conv_b, conv_b, w_down,
      lnw.reshape(1, -1), lnb.reshape(1, -1))


def kernel(x_prompt, x_sample, cache_k, cache_v, state_wkv, state_shift, state_conv, page_table, rel_bias_table, w_in, mu_shift, decay_base, decay_up, iclr_base, iclr_up, gate_up, kk_scale, k_iclr_mix, bonus_rk, gn_w, gn_b, lam_q1, lam_k1, lam_q2, lam_k2, subln_w, w_branch_a, w_branch_b, w_out, ln1_w, ln1_b, w_up, conv_w, conv_b, w_down, ln2_w, ln2_b):
    assert w_in.shape[0] == DEPTH == 1
    l = 0
    lam_init = 0.8 - 0.6 * math.exp(-0.3 * l)
    bp, seq, _ = x_prompt.shape
    bs, n_new, _ = x_sample.shape
    page = cache_k.shape[2]
    assert page == LANES and page >= MAX_DISTANCE and ATT_T >= MAX_DISTANCE and n_new <= SAMPLE_T_PAD

    w_in_b = w_in[l].astype(BF16)
    wa, wb, wo = w_branch_a[l].astype(BF16), w_branch_b[l].astype(BF16), w_out[l].astype(BF16)
    w_up_b, w_down_b = w_up[l].astype(BF16), w_down[l].astype(BF16)
    tm_params = (mu_shift[l], decay_base[l], decay_up[l], iclr_base[l], iclr_up[l], gate_up[l], kk_scale[l],
                 k_iclr_mix[l], bonus_rk[l].reshape(-1), gn_w[l], gn_b[l])
    lam_params = [p[l].reshape(1, -1) for p in (lam_q1, lam_k1, lam_q2, lam_k2)]
    subln = subln_w[l].reshape(1, -1)

    mp = bp * seq
    xp2 = x_prompt.reshape(mp, D_MODEL)
    cols, q, k, v, gates = _proj(xp2, w_in_b)
    cols3 = cols.reshape(bp, seq, N_SHIFT)
    ya, wkv_p = _timemix(cols3, jnp.zeros((bp, N_SHIFT), F32),
                         jnp.zeros((bp, A_HEADS, A_HEAD_DIM, A_HEAD_DIM), F32), tm_params,
                         bb=bp, tc=LANES, n_steps=LANES)
    r3 = lambda a: a.reshape(bp, seq, B_WIDTH)
    ob = _attn_prompt(r3(q), r3(k), r3(v), _prompt_bias(rel_bias_table), lam_params, subln, lam_init)
    x1 = _merge(xp2, ya.reshape(mp, A_WIDTH), ob.reshape(mp, B_WIDTH), gates, wa, wb, wo, ln1_w[l], ln1_b[l])
    y_p, tu, tg = _ffn_prompt(x1.reshape(bp, seq, D_MODEL), w_up_b, conv_w[l], conv_b[l], w_down_b,
                              ln2_w[l], ln2_b[l])
    nbuf = CONV_WIDTH - 1
    k_prompt = k.reshape(1, bp, seq, B_HEADS, 2, B_HEAD_DIM)
    v_prompt = v.reshape(1, bp, seq, B_HEADS, B_V_DIM)
    shift_prompt = cols3[:, seq - 1][None]
    conv_prompt = jnp.concatenate([tu[:, -1, SUBLANES - nbuf:], tg[:, -1, SUBLANES - nbuf:]], axis=-1)[None]

    tp = SAMPLE_T_PAD
    xs8 = jnp.pad(x_sample, ((0, 0), (0, tp - n_new), (0, 0))).reshape(bs * tp, D_MODEL)
    cols, q, k, v, gates = _proj(xs8, w_in_b)
    cols3 = cols.reshape(bs, tp, N_SHIFT)
    ya, wkv_s = _timemix(cols3, state_shift[l], state_wkv[l], tm_params, bb=4, tc=tp, n_steps=n_new)
    r3 = lambda a: a.reshape(bs, tp, B_WIDTH)
    n_phys = cache_k.shape[1]
    ob = _attn_sample(r3(q), r3(k), r3(v), cache_k[l].reshape(n_phys, page, B_WIDTH),
                      cache_v[l].reshape(n_phys, page, B_WIDTH), page_table,
                      _sample_bias(rel_bias_table, n_new, page), lam_params, subln, lam_init, n_new)
    x1 = _merge(xs8, ya.reshape(bs * tp, A_WIDTH), ob.reshape(bs * tp, B_WIDTH), gates, wa, wb, wo,
                ln1_w[l], ln1_b[l])
    x1_pm = jnp.swapaxes(x1.reshape(bs, tp, D_MODEL)[:, :n_new], 0, 1).reshape(n_new * bs, D_MODEL)
    buf_pm = jnp.swapaxes(state_conv[l], 0, 1).reshape(nbuf * bs, 2 * FFN_DIM)
    y_pm, tu, tg = _ffn_sample(x1_pm, buf_pm, w_up_b, conv_w[l], conv_b[l], w_down_b, ln2_w[l], ln2_b[l],
                               n_new, bs)
    y_s = jnp.swapaxes(y_pm.reshape(n_new, bs, D_MODEL), 0, 1)
    conv_sample = jnp.swapaxes(jnp.concatenate([tu, tg], axis=-1).reshape(nbuf, bs, 2 * FFN_DIM), 0, 1)[None]
    k_sample = k.reshape(bs, tp, B_HEADS, 2, B_HEAD_DIM)[:, :n_new][None]
    v_sample = v.reshape(bs, tp, B_HEADS, B_V_DIM)[:, :n_new][None]
    shift_sample = cols3[:, n_new - 1][None]

    return (y_p, y_s, k_prompt, v_prompt, wkv_p[None], shift_prompt, conv_prompt,
            k_sample, v_sample, wkv_s[None], shift_sample, conv_sample)
```

```python
import functools
import math

import jax
import jax.numpy as jnp
from jax import lax
from jax.experimental import pallas as pl
from jax.experimental.pallas import tpu as pltpu

F32 = jnp.float32
BF16 = jnp.bfloat16

D_MODEL = 1024
DEPTH = 1
A_WIDTH = D_MODEL // 2
A_HEAD_DIM = 64
A_HEADS = A_WIDTH // A_HEAD_DIM
LORA_DECAY = 64
LORA_ICLR = 64
LORA_GATE = 128
N_SHIFT = 3 * A_WIDTH + LORA_DECAY + LORA_ICLR + LORA_GATE
B_WIDTH = D_MODEL // 2
B_HEAD_DIM = 64
B_V_DIM = 2 * B_HEAD_DIM
B_HEADS = B_WIDTH // B_V_DIM
N_COLS = N_SHIFT + 3 * B_WIDTH + 2 * D_MODEL
NUM_BUCKETS = 32
MAX_DISTANCE = 128
FFN_DIM = ((8 * D_MODEL // 3 + 255) // 256) * 256
CONV_WIDTH = 3
LN_EPS = 1e-5
GN_EPS = 64e-5
RMS_EPS = 1e-5
DEEPNORM_ALPHA = (2 * DEPTH) ** 0.25

LANES = 128
SUBLANES = 8
VMEM_LIMIT = 56 * 1024 * 1024

NEG_BIG = -1e30
HEAD_PAIRS = A_HEADS // 2

PROJ_TM = 256
ATT_T = 256
PAGES_PER_STEP = 8
SAMPLE_T_PAD = SUBLANES
MERGE_TM = 512
FFN_TM = 512
FFN_TF = FFN_DIM // 2


def _cparams(*sem):
    return pltpu.CompilerParams(dimension_semantics=sem, vmem_limit_bytes=VMEM_LIMIT)


def _full(shape):
    n = len(shape)
    return pl.BlockSpec(shape, lambda *_: (0,) * n)


def _proj_kernel(x_ref, w_ref, cols_ref, q_ref, k_ref, v_ref, gates_ref):
    xb = x_ref[...].astype(BF16)

    def mm(lo, hi):
        return jnp.dot(xb, w_ref[:, lo:hi], preferred_element_type=F32)

    o = N_SHIFT
    cols_ref[...] = mm(0, o)
    q_ref[...] = mm(o, o + B_WIDTH) * (B_HEAD_DIM ** -0.5)
    k_ref[...] = mm(o + B_WIDTH, o + 2 * B_WIDTH)
    v_ref[...] = mm(o + 2 * B_WIDTH, o + 3 * B_WIDTH)
    gates_ref[...] = mm(o + 3 * B_WIDTH, N_COLS)


def _proj(x2d, w_in_bf16):
    m = x2d.shape[0]
    tm = min(PROJ_TM, m)
    widths = (N_SHIFT, B_WIDTH, B_WIDTH, B_WIDTH, 2 * D_MODEL)
    return pl.pallas_call(
        _proj_kernel,
        grid=(m // tm,),
        in_specs=[pl.BlockSpec((tm, D_MODEL), lambda i: (i, 0)), _full((D_MODEL, N_COLS))],
        out_specs=[pl.BlockSpec((tm, w), lambda i: (i, 0)) for w in widths],
        out_shape=[jax.ShapeDtypeStruct((m, w), F32) for w in widths],
        compiler_params=_cparams("arbitrary"),
    )(x2d, w_in_bf16)


def _pair_lo_mask():
    return lax.broadcasted_iota(jnp.int32, (1, LANES), 1) < A_HEAD_DIM


def _head_sum(x):
    lo = _pair_lo_mask()
    out = []
    for p in range(HEAD_PAIRS):
        xp = x[:, p * LANES:(p + 1) * LANES]
        s_lo = jnp.sum(jnp.where(lo, xp, 0.0), axis=-1, keepdims=True)
        s_hi = jnp.sum(jnp.where(lo, 0.0, xp), axis=-1, keepdims=True)
        out.append(jnp.where(lo, s_lo, s_hi))
    return jnp.concatenate(out, axis=-1)


def _softplus(z):
    return jnp.maximum(z, 0.0) + jnp.log1p(jnp.exp(-jnp.abs(z)))


def _sigmoid(z):
    return 1.0 / (1.0 + jnp.exp(-z))


def _timemix_kernel(cols_ref, shift0_ref, wkv0_ref, mu_ref, dbase_ref, dup_ref, ibase_ref, iup_ref, gup_ref,
                    kks_ref, kim_ref, bonus_ref, gnw_ref, gnb_ref,
                    y_ref, wkv_ref,
                    s_scr, carry_scr, r_scr, w_scr, k_scr, kk_scr, ak_scr, g_scr, bon_scr, vpad_scr, vt_scr, yt_scr,
                    *, bb, tc, n_steps):
    c = pl.program_id(1)
    aw = A_WIDTH

    @pl.when(c == 0)
    def _init():
        s_scr[...] = wkv0_ref[...].reshape(bb * HEAD_PAIRS, A_HEAD_DIM, LANES)
        carry_scr[...] = shift0_ref[...]
        vpad_scr[...] = jnp.zeros_like(vpad_scr)

    rows = lax.broadcasted_iota(jnp.int32, (tc, 1), 0)
    for b in range(bb):
        cols = cols_ref[b]
        prev = jnp.where(rows == 0, carry_scr[b], pltpu.roll(cols, 1, axis=0))
        carry_scr[b] = cols[n_steps - 1:n_steps, :]
        h = cols + (prev - cols) * mu_ref[...]
        r, k, v = h[:, 0:aw], h[:, aw:2 * aw], h[:, 2 * aw:3 * aw]
        h_wa = h[:, 3 * aw:3 * aw + LANES]
        h_g = h[:, 3 * aw + LANES:3 * aw + 2 * LANES]
        z = dbase_ref[...] + jnp.dot(jnp.tanh(h_wa).astype(BF16), dup_ref[...], preferred_element_type=F32)
        w_raw = -_softplus(-z) - 0.5
        decay = jnp.exp(-jnp.exp(w_raw))
        a = _sigmoid(ibase_ref[...] + jnp.dot(h_wa.astype(BF16), iup_ref[...], preferred_element_type=F32))
        g = jnp.dot(_sigmoid(h_g).astype(BF16), gup_ref[...], preferred_element_type=F32)
        kk = k * kks_ref[...]
        kk = kk / jnp.maximum(jnp.sqrt(_head_sum(kk * kk)), 1e-12)
        k = k * (1.0 + (a - 1.0) * kim_ref[...])
        for ref, val in ((r_scr, r), (w_scr, decay), (k_scr, k), (kk_scr, kk), (ak_scr, kk * a)):
            for p in range(HEAD_PAIRS):
                ref[b * HEAD_PAIRS + p] = val[:, p * LANES:(p + 1) * LANES]
        g_scr[b] = g
        bon_scr[b] = _head_sum(r * k * bonus_ref[...]) * v
        if tc == LANES:
            vt_scr[b] = v.T
        else:
            vpad_scr[b, 0:tc, :] = v
            vt_scr[b] = vpad_scr[b].T
    yt_scr[...] = jnp.zeros_like(yt_scr)

    lane = lax.broadcasted_iota(jnp.int32, (1, LANES), 1)
    lo = lane < A_HEAD_DIM
    hd = A_HEAD_DIM

    def seg_sum(x):
        s_lo = jnp.sum(jnp.where(lo, x, 0.0), axis=-1, keepdims=True)
        s_hi = jnp.sum(jnp.where(lo, 0.0, x), axis=-1, keepdims=True)
        return s_lo, s_hi

    n_sub = min(SUBLANES, n_steps)

    def block(i, carry):
        base = pl.multiple_of(i * SUBLANES, SUBLANES)
        for j in range(n_sub):
            tmask = lane == base + j
            for b in range(bb):
                for p in range(HEAD_PAIRS):
                    idx = b * HEAD_PAIRS + p
                    ra, rb = p * LANES, p * LANES + hd
                    r_t, w_t, k_t, kk_t, ak_t = (
                        jnp.broadcast_to(ref[idx, pl.ds(base, SUBLANES), :][j:j + 1, :], (hd, LANES))
                        for ref in (r_scr, w_scr, k_scr, kk_scr, ak_scr))
                    v_a = jnp.sum(jnp.where(tmask, vt_scr[b, ra:ra + hd, :], 0.0), axis=-1, keepdims=True)
                    v_b = jnp.sum(jnp.where(tmask, vt_scr[b, rb:rb + hd, :], 0.0), axis=-1, keepdims=True)
                    v_col = jnp.where(lo, v_a, v_b)
                    s = s_scr[idx]
                    u_a, u_b = seg_sum(s * kk_t)
                    s = s * w_t - jnp.where(lo, u_a, u_b) * ak_t + v_col * k_t
                    s_scr[idx] = s
                    y_a, y_b = seg_sum(s * r_t)
                    yt_scr[b, ra:ra + hd, :] = jnp.where(tmask, y_a, yt_scr[b, ra:ra + hd, :])
                    yt_scr[b, rb:rb + hd, :] = jnp.where(tmask, y_b, yt_scr[b, rb:rb + hd, :])
        return carry

    lax.fori_loop(0, n_steps // n_sub, block, 0)

    inv_n = 1.0 / A_HEAD_DIM
    for b in range(bb):
        y = yt_scr[b].T[0:tc, :] + bon_scr[b]
        mu = _head_sum(y) * inv_n
        d = y - mu
        var = _head_sum(d * d) * inv_n
        y = d * lax.rsqrt(var + GN_EPS) * gnw_ref[...] + gnb_ref[...]
        y_ref[b] = y * g_scr[b]

    @pl.when(c == pl.num_programs(1) - 1)
    def _fin():
        wkv_ref[...] = s_scr[...].reshape(bb, HEAD_PAIRS, A_HEAD_DIM, LANES)


def _pack_pairs(wkv):
    b = wkv.shape[0]
    x = wkv.reshape(b, HEAD_PAIRS, 2, A_HEAD_DIM, A_HEAD_DIM)
    return jnp.swapaxes(x, 2, 3).reshape(b, HEAD_PAIRS, A_HEAD_DIM, LANES)


def _unpack_pairs(x):
    b = x.shape[0]
    x = x.reshape(b, HEAD_PAIRS, A_HEAD_DIM, 2, A_HEAD_DIM)
    return jnp.swapaxes(x, 2, 3).reshape(b, A_HEADS, A_HEAD_DIM, A_HEAD_DIM)


def _timemix(cols, shift0, wkv0, tm_params, *, bb, tc, n_steps):
    bsz, seq, _ = cols.shape
    nb, nc = bsz // bb, seq // tc
    vec = lambda a: a.reshape(1, -1)
    mu, dbase, dup, ibase, iup, gup, kks, kim, bonus, gnw, gnb = tm_params
    zpad = jnp.zeros((LORA_DECAY, A_WIDTH), F32)
    dup_p = jnp.concatenate([dup, zpad], 0).astype(BF16)
    iup_p = jnp.concatenate([zpad, iup], 0).astype(BF16)
    params = [vec(mu), vec(dbase), dup_p, vec(ibase), iup_p, gup.astype(BF16), vec(kks), vec(kim), vec(bonus),
              vec(gnw), vec(gnb)]
    scr = lambda *s: pltpu.VMEM(s, F32)
    y, wkv = pl.pallas_call(
        functools.partial(_timemix_kernel, bb=bb, tc=tc, n_steps=n_steps),
        grid=(nb, nc),
        in_specs=[pl.BlockSpec((bb, tc, N_SHIFT), lambda i, c: (i, c, 0)),
                  pl.BlockSpec((bb, 1, N_SHIFT), lambda i, c: (i, 0, 0)),
                  pl.BlockSpec((bb, HEAD_PAIRS, A_HEAD_DIM, LANES), lambda i, c: (i, 0, 0, 0))]
                 + [_full(p.shape) for p in params],
        out_specs=[pl.BlockSpec((bb, tc, A_WIDTH), lambda i, c: (i, c, 0)),
                   pl.BlockSpec((bb, HEAD_PAIRS, A_HEAD_DIM, LANES), lambda i, c: (i, 0, 0, 0))],
        out_shape=[jax.ShapeDtypeStruct((bsz, seq, A_WIDTH), F32),
                   jax.ShapeDtypeStruct((bsz, HEAD_PAIRS, A_HEAD_DIM, LANES), F32)],
        scratch_shapes=[scr(bb * HEAD_PAIRS, A_HEAD_DIM, LANES), scr(bb, 1, N_SHIFT)]
                       + [scr(bb * HEAD_PAIRS, tc, LANES)] * 5 + [scr(bb, tc, A_WIDTH)] * 2
                       + [scr(bb, LANES, A_WIDTH), scr(bb, A_WIDTH, LANES), scr(bb, A_WIDTH, LANES)],
        compiler_params=_cparams("arbitrary", "arbitrary"),
    )(cols, shift0.reshape(bsz, 1, N_SHIFT), _pack_pairs(wkv0), *params)
    return y, _unpack_pairs(wkv)


def _bucket(n):
    max_exact = NUM_BUCKETS // 2
    nf = jnp.maximum(n, 1).astype(F32)
    large = max_exact + (jnp.log(nf / max_exact) / math.log(MAX_DISTANCE / max_exact)
                         * (NUM_BUCKETS - max_exact)).astype(jnp.int32)
    return jnp.where(n < max_exact, n, jnp.minimum(large, NUM_BUCKETS - 1))


def _prompt_bias_kernel(table_ref, out_ref):
    h = pl.program_id(0)
    t = ATT_T
    rr = lax.broadcasted_iota(jnp.int32, (t, t), 0)
    cc = lax.broadcasted_iota(jnp.int32, (t, t), 1)
    for i in range(3):
        d = (2 - i) * t + rr - cc
        bkt = _bucket(jnp.maximum(d, 0))
        bias = jnp.zeros((t, t), F32)
        for bk in range(NUM_BUCKETS):
            bias = jnp.where(bkt == bk, table_ref[bk, h], bias)
        out_ref[0, i] = jnp.where(d >= 0, bias, NEG_BIG)


def _prompt_bias(table):
    t = ATT_T
    return pl.pallas_call(
        _prompt_bias_kernel,
        grid=(B_HEADS,),
        in_specs=[pl.BlockSpec(memory_space=pltpu.SMEM)],
        out_specs=pl.BlockSpec((1, 3, t, t), lambda h: (h, 0, 0, 0)),
        out_shape=jax.ShapeDtypeStruct((B_HEADS, 3, t, t), F32),
        compiler_params=_cparams("arbitrary"),
    )(table)


def _sample_rows(n_new):
    return n_new * 2 * B_HEADS


def _sample_bias_kernel(table_ref, out_ref, *, n_new, page):
    nr = _sample_rows(n_new)
    rr = lax.broadcasted_iota(jnp.int32, (nr, LANES), 0)
    cc = lax.broadcasted_iota(jnp.int32, (nr, LANES), 1)
    tok = rr % n_new
    head = rr // (2 * n_new)

    def lookup(d):
        bkt = _bucket(jnp.maximum(d, 0))
        bias = jnp.zeros((nr, LANES), F32)
        for hh in range(B_HEADS):
            for bk in range(NUM_BUCKETS):
                bias = jnp.where((bkt == bk) & (head == hh), table_ref[bk, hh], bias)
        return bias

    far = lookup(jnp.full((nr, LANES), MAX_DISTANCE, jnp.int32))
    out_ref[0] = far
    out_ref[1] = lookup(tok + page - cc)
    d_self = tok - cc
    out_ref[2] = jnp.where((d_self >= 0) & (cc < n_new), lookup(d_self), NEG_BIG)


def _sample_bias(table, n_new, page):
    nr = _sample_rows(n_new)
    return pl.pallas_call(
        functools.partial(_sample_bias_kernel, n_new=n_new, page=page),
        in_specs=[pl.BlockSpec(memory_space=pltpu.SMEM)],
        out_specs=pl.BlockSpec(memory_space=pltpu.VMEM),
        out_shape=jax.ShapeDtypeStruct((3, nr, LANES), F32),
    )(table)


def _lambda(lq1_ref, lk1_ref, lq2_ref, lk2_ref, lam_init):
    s1 = jnp.sum(lq1_ref[...] * lk1_ref[...], axis=-1, keepdims=True)
    s2 = jnp.sum(lq2_ref[...] * lk2_ref[...], axis=-1, keepdims=True)
    return jnp.exp(s1) - jnp.exp(s2) + lam_init


def _nt_dot(a, b):
    return lax.dot_general(a, b, (((1,), (1,)), ((), ())), preferred_element_type=F32)


def _attn_prompt_kernel(q_ref, k_ref, v_ref, bias_ref, lq1_ref, lk1_ref, lq2_ref, lk2_ref, subln_ref, o_ref,
                        acc1_scr, acc2_scr, *, lam_init):
    i = pl.program_id(2)
    t = ATT_T
    lane = lax.broadcasted_iota(jnp.int32, (1, LANES), 1)
    first = lane < B_HEAD_DIM
    q = q_ref[0]
    q1 = jnp.where(first, q, 0.0).astype(BF16)
    q2 = jnp.where(first, 0.0, q).astype(BF16)
    acc1_scr[...] = jnp.zeros_like(acc1_scr)
    acc2_scr[...] = jnp.zeros_like(acc2_scr)

    def online(s, m, l, acc_scr, vt):
        m_new = jnp.maximum(m, jnp.max(s, axis=-1, keepdims=True))
        p = jnp.exp(s - m_new)
        alpha = jnp.exp(m - m_new)
        l = alpha * l + jnp.sum(p, axis=-1, keepdims=True)
        acc_scr[...] = alpha * acc_scr[...] + jnp.dot(p.astype(BF16), vt, preferred_element_type=F32)
        return m_new, l

    def tile(j, carry):
        m1, l1, m2, l2 = carry
        off = pl.multiple_of(j * t, t)
        kt = k_ref[0, pl.ds(off, t), :].astype(BF16)
        vt = v_ref[0, pl.ds(off, t), :].astype(BF16)
        bias = bias_ref[0, jnp.clip(j - i + 2, 0, 2)]
        m1, l1 = online(_nt_dot(q1, kt) + bias, m1, l1, acc1_scr, vt)
        m2, l2 = online(_nt_dot(q2, kt) + bias, m2, l2, acc2_scr, vt)
        return m1, l1, m2, l2

    m0 = jnp.full((t, 1), NEG_BIG, F32)
    l0 = jnp.zeros((t, 1), F32)
    _, l1, _, l2 = lax.fori_loop(0, i + 1, tile, (m0, l0, m0, l0))
    lam = _lambda(lq1_ref, lk1_ref, lq2_ref, lk2_ref, lam_init)
    o = acc1_scr[...] * (1.0 / l1) - acc2_scr[...] * (lam / l2)
    ms = jnp.sum(o * o, axis=-1, keepdims=True) * (1.0 / B_V_DIM)
    o_ref[0] = o * lax.rsqrt(ms + RMS_EPS) * subln_ref[...] * (1.0 - lam_init)


def _attn_prompt(q, k, v, bias, lam_params, subln, lam_init):
    bsz, seq, _ = q.shape
    t = ATT_T
    lam_specs = [_full((1, B_HEAD_DIM))] * 4
    return pl.pallas_call(
        functools.partial(_attn_prompt_kernel, lam_init=lam_init),
        grid=(bsz, B_HEADS, seq // t),
        in_specs=[pl.BlockSpec((1, t, B_V_DIM), lambda b, h, i: (b, i, h)),
                  pl.BlockSpec((1, seq, B_V_DIM), lambda b, h, i: (b, 0, h)),
                  pl.BlockSpec((1, seq, B_V_DIM), lambda b, h, i: (b, 0, h)),
                  pl.BlockSpec((1, 3, t, t), lambda b, h, i: (h, 0, 0, 0))]
                 + lam_specs + [_full((1, B_V_DIM))],
        out_specs=pl.BlockSpec((1, t, B_V_DIM), lambda b, h, i: (b, i, h)),
        out_shape=jax.ShapeDtypeStruct((bsz, seq, B_WIDTH), F32),
        scratch_shapes=[pltpu.VMEM((t, B_V_DIM), F32)] * 2,
        compiler_params=_cparams("arbitrary", "arbitrary", "arbitrary"),
    )(q, k, v, bias, *lam_params, subln)


def _attn_sample_kernel(pt_ref, q_ref, kn_ref, vn_ref, sb_ref, lq1_ref, lk1_ref, lq2_ref, lk2_ref, subln_ref,
                        *rest, n_new, lam_init):
    pg = PAGES_PER_STEP
    k_refs, v_refs = rest[:pg], rest[pg:2 * pg]
    o_ref = rest[2 * pg]
    qb_scr, m_scr, l_scr, acc_scr, kn_scr, vn_scr = rest[2 * pg + 1:]
    del pt_ref
    g = pl.program_id(1)
    ng = pl.num_programs(1)
    nr = _sample_rows(n_new)
    rph = 2 * n_new
    page = kn_scr.shape[0]

    @pl.when(g == 0)
    def _init():
        rr = lax.broadcasted_iota(jnp.int32, (rph, B_WIDTH), 0)
        cc = lax.broadcasted_iota(jnp.int32, (rph, B_WIDTH), 1)
        q = jnp.where(rr < n_new, q_ref[0], 0.0)
        q2 = q + pltpu.roll(q, n_new, axis=0)
        for h in range(B_HEADS):
            sel = (cc // B_HEAD_DIM) == 2 * h + rr // n_new
            qb_scr[h * rph:(h + 1) * rph, :] = jnp.where(sel, q2, 0.0)
        m_scr[...] = jnp.full_like(m_scr, NEG_BIG)
        l_scr[...] = jnp.zeros_like(l_scr)
        acc_scr[...] = jnp.zeros_like(acc_scr)
        kn_scr[...] = jnp.zeros_like(kn_scr)
        vn_scr[...] = jnp.zeros_like(vn_scr)
        kn_scr[0:SAMPLE_T_PAD, :] = kn_ref[0]
        vn_scr[0:SAMPLE_T_PAD, :] = vn_ref[0]

    def update(s_list, v_list):
        m = m_scr[...]
        m_new = m
        for s in s_list:
            m_new = jnp.maximum(m_new, jnp.max(s, axis=-1, keepdims=True))
        alpha = jnp.exp(m - m_new)
        l = alpha * l_scr[...]
        acc = [alpha[h * rph:(h + 1) * rph] * acc_scr[h * rph:(h + 1) * rph, :] for h in range(B_HEADS)]
        for s, v_heads in zip(s_list, v_list):
            p = jnp.exp(s - m_new)
            l = l + jnp.sum(p, axis=-1, keepdims=True)
            for h in range(B_HEADS):
                acc[h] = acc[h] + jnp.dot(p[h * rph:(h + 1) * rph, :].astype(BF16), v_heads[h],
                                          preferred_element_type=F32)
        m_scr[...] = m_new
        l_scr[...] = l
        for h in range(B_HEADS):
            acc_scr[h * rph:(h + 1) * rph, :] = acc[h]

    qb = qb_scr[...].astype(BF16)
    far = sb_ref[0]
    s_list, v_list = [], []
    for i in range(pg):
        bias = far
        if i == pg - 1:
            bias = sb_ref[jnp.where(g == ng - 1, 1, 0)]
        s_list.append(jnp.dot(qb, k_refs[i][0].astype(BF16), preferred_element_type=F32) + bias)
        v_list.append([v_refs[i][0, pl.ds(h, page, stride=B_HEADS), :].astype(BF16) for h in range(B_HEADS)])
    update(s_list, v_list)

    @pl.when(g == ng - 1)
    def _fin():
        vn = vn_scr[...].astype(BF16)
        update([_nt_dot(qb, kn_scr[...].astype(BF16)) + sb_ref[2]],
               [[vn[:, h * B_V_DIM:(h + 1) * B_V_DIM] for h in range(B_HEADS)]])
        lam = _lambda(lq1_ref, lk1_ref, lq2_ref, lk2_ref, lam_init)
        rr = lax.broadcasted_iota(jnp.int32, (nr, 1), 0)
        coef = jnp.where(rr % rph < n_new, 1.0, -lam) / l_scr[...]
        w = acc_scr[...] * coef
        parts = []
        for h in range(B_HEADS):
            wh = w[h * rph:(h + 1) * rph, :]
            oh = wh + pltpu.roll(wh, n_new, axis=0)
            ms = jnp.sum(oh * oh, axis=-1, keepdims=True) * (1.0 / B_V_DIM)
            parts.append(oh * lax.rsqrt(ms + RMS_EPS) * subln_ref[...] * (1.0 - lam_init))
        o_ref[0] = jnp.concatenate(parts, axis=-1)


def _attn_sample(q, k_new, v_new, cache_k, cache_v, page_table, sbias, lam_params, subln, lam_init, n_new):
    bsz = q.shape[0]
    n_pages = page_table.shape[1]
    page = cache_k.shape[2]
    pg = PAGES_PER_STEP
    nr = _sample_rows(n_new)
    assert 2 * n_new == SAMPLE_T_PAD == SUBLANES
    tok_spec = pl.BlockSpec((1, SAMPLE_T_PAD, B_WIDTH), lambda b, g, pt: (b, 0, 0))

    def page_spec(i):
        return pl.BlockSpec((1, B_WIDTH, page), lambda b, g, pt: (pt[b, g * pg + i], 0, 0))

    const = lambda shape: pl.BlockSpec(shape, lambda b, g, pt: (0,) * len(shape))
    grid_spec = pltpu.PrefetchScalarGridSpec(
        num_scalar_prefetch=1,
        grid=(bsz, n_pages // pg),
        in_specs=[tok_spec, tok_spec, tok_spec, const((3, nr, LANES))]
                 + [const((1, B_HEAD_DIM))] * 4 + [const((1, B_V_DIM))]
                 + [page_spec(i) for i in range(pg)] * 2,
        out_specs=tok_spec,
        scratch_shapes=[pltpu.VMEM((nr, B_WIDTH), F32), pltpu.VMEM((nr, 1), F32), pltpu.VMEM((nr, 1), F32),
                        pltpu.VMEM((nr, B_V_DIM), F32), pltpu.VMEM((page, B_WIDTH), F32),
                        pltpu.VMEM((page, B_WIDTH), F32)],
    )
    return pl.pallas_call(
        functools.partial(_attn_sample_kernel, n_new=n_new, lam_init=lam_init),
        grid_spec=grid_spec,
        out_shape=jax.ShapeDtypeStruct((bsz, SAMPLE_T_PAD, B_WIDTH), F32),
        compiler_params=_cparams("arbitrary", "arbitrary"),
    )(page_table, q, k_new, v_new, sbias, *lam_params, subln, *([cache_k] * pg), *([cache_v] * pg))


def _layer_norm(z, w, b):
    inv_d = 1.0 / D_MODEL
    mu = jnp.sum(z, axis=-1, keepdims=True) * inv_d
    d = z - mu
    var = jnp.sum(d * d, axis=-1, keepdims=True) * inv_d
    return d * lax.rsqrt(var + LN_EPS) * w + b


def _merge_kernel(x_ref, ya_ref, ob_ref, gates_ref, wa_ref, wb_ref, wo_ref, lnw_ref, lnb_ref, o_ref):
    ga = _sigmoid(gates_ref[:, 0:D_MODEL])
    gb = _sigmoid(gates_ref[:, D_MODEL:2 * D_MODEL])
    merged = (ga * jnp.dot(ya_ref[...].astype(BF16), wa_ref[...], preferred_element_type=F32)
              + gb * jnp.dot(ob_ref[...].astype(BF16), wb_ref[...], preferred_element_type=F32))
    mix = jnp.dot(merged.astype(BF16), wo_ref[...], preferred_element_type=F32)
    o_ref[...] = _layer_norm(DEEPNORM_ALPHA * x_ref[...] + mix, lnw_ref[...], lnb_ref[...])


def _merge(x2d, ya, ob, gates, wa, wb, wo, lnw, lnb):
    m = x2d.shape[0]
    tm = min(MERGE_TM, m)
    row = lambda w: pl.BlockSpec((tm, w), lambda i: (i, 0))
    return pl.pallas_call(
        _merge_kernel,
        grid=(m // tm,),
        in_specs=[row(D_MODEL), row(A_WIDTH), row(B_WIDTH), row(2 * D_MODEL),
                  _full(wa.shape), _full(wb.shape), _full(wo.shape), _full((1, D_MODEL)), _full((1, D_MODEL))],
        out_specs=row(D_MODEL),
        out_shape=jax.ShapeDtypeStruct((m, D_MODEL), F32),
        compiler_params=_cparams("arbitrary"),
    )(x2d, ya, ob, gates, wa, wb, wo, lnw.reshape(1, -1), lnb.reshape(1, -1))


def _gelu(x):
    return 0.5 * x * (1.0 + lax.erf(x * (2.0 ** -0.5)))


def _ffn_prompt_kernel(x_ref, wu_ref, wg_ref, cwu_ref, cwg_ref, cbu_ref, cbg_ref, wd_ref, lnw_ref, lnb_ref,
                       o_ref, tu_ref, tg_ref, acc_scr, cu_scr, cg_scr):
    i, j = pl.program_id(1), pl.program_id(2)
    tm = x_ref.shape[1]
    rows = lax.broadcasted_iota(jnp.int32, (tm, 1), 0)
    xb = x_ref[0].astype(BF16)

    @pl.when(i == 0)
    def _reset():
        cu_scr[j] = jnp.zeros((SUBLANES, FFN_TF), F32)
        cg_scr[j] = jnp.zeros((SUBLANES, FFN_TF), F32)

    def conv(w_ref, cw_ref, cb_ref, carry_scr, tail_ref):
        h = jnp.dot(xb, w_ref[...], preferred_element_type=F32)
        carry = carry_scr[j]
        c6, c7 = carry[SUBLANES - 2:SUBLANES - 1, :], carry[SUBLANES - 1:SUBLANES, :]
        h1 = jnp.where(rows == 0, c7, pltpu.roll(h, 1, axis=0))
        h2 = jnp.where(rows == 0, c6, jnp.where(rows == 1, c7, pltpu.roll(h, 2, axis=0)))
        tail = h[tm - SUBLANES:tm, :]
        carry_scr[j] = tail
        tail_ref[0, 0] = tail
        return cb_ref[...] + (cw_ref[0:1, :] * h2 + cw_ref[1:2, :] * h1 + cw_ref[2:3, :] * h)

    u = conv(wu_ref, cwu_ref, cbu_ref, cu_scr, tu_ref)
    g = conv(wg_ref, cwg_ref, cbg_ref, cg_scr, tg_ref)
    part = jnp.dot((_gelu(g) * u).astype(BF16), wd_ref[...], preferred_element_type=F32)

    @pl.when(j == 0)
    def _first():
        acc_scr[...] = part

    @pl.when(j > 0)
    def _rest():
        acc_scr[...] += part

    @pl.when(j == pl.num_programs(2) - 1)
    def _fin():
        o_ref[0] = _layer_norm(DEEPNORM_ALPHA * x_ref[0] + acc_scr[...], lnw_ref[...], lnb_ref[...])


def _ffn_prompt(x, w_up, conv_w, conv_b, w_down, lnw, lnb):
    bsz, seq, _ = x.shape
    tm, tf = FFN_TM, FFN_TF
    nf = FFN_DIM // tf
    nt = seq // tm
    conv_b = conv_b.reshape(1, -1)
    ucol = lambda r: pl.BlockSpec((r, tf), lambda b, i, j: (0, j))
    gcol = lambda r: pl.BlockSpec((r, tf), lambda b, i, j: (0, nf + j))
    tail = pl.BlockSpec((1, 1, SUBLANES, tf), lambda b, i, j: (b, i, 0, j))
    return pl.pallas_call(
        _ffn_prompt_kernel,
        grid=(bsz, nt, nf),
        in_specs=[pl.BlockSpec((1, tm, D_MODEL), lambda b, i, j: (b, i, 0)),
                  ucol(D_MODEL), gcol(D_MODEL), ucol(CONV_WIDTH), gcol(CONV_WIDTH), ucol(1), gcol(1),
                  pl.BlockSpec((tf, D_MODEL), lambda b, i, j: (j, 0)),
                  pl.BlockSpec((1, D_MODEL), lambda b, i, j: (0, 0)),
                  pl.BlockSpec((1, D_MODEL), lambda b, i, j: (0, 0))],
        out_specs=[pl.BlockSpec((1, tm, D_MODEL), lambda b, i, j: (b, i, 0)), tail, tail],
        out_shape=[jax.ShapeDtypeStruct((bsz, seq, D_MODEL), F32),
                   jax.ShapeDtypeStruct((bsz, nt, SUBLANES, FFN_DIM), F32),
                   jax.ShapeDtypeStruct((bsz, nt, SUBLANES, FFN_DIM), F32)],
        scratch_shapes=[pltpu.VMEM((tm, D_MODEL), F32), pltpu.VMEM((nf, SUBLANES, tf), F32),
                        pltpu.VMEM((nf, SUBLANES, tf), F32)],
        compiler_params=_cparams("arbitrary", "arbitrary", "arbitrary"),
    )(x, w_up, w_up, conv_w, conv_w, conv_b, conv_b, w_down, lnw.reshape(1, -1), lnb.reshape(1, -1))


def _ffn_sample_kernel(x_ref, bu_ref, bg_ref, wu_ref, wg_ref, cwu_ref, cwg_ref, cbu_ref, cbg_ref, wd_ref,
                       lnw_ref, lnb_ref, o_ref, tu_ref, tg_ref, acc_scr, *, n_new, bsz):
    j = pl.program_id(0)
    m = n_new * bsz
    nbuf = (CONV_WIDTH - 1) * bsz
    xb = x_ref[...].astype(BF16)

    def conv(w_ref, cw_ref, cb_ref, buf_ref, tail_ref):
        h = jnp.dot(xb, w_ref[...], preferred_element_type=F32)
        hp = jnp.concatenate([buf_ref[...], h], axis=0)
        tail_ref[...] = hp[m:m + nbuf, :]
        return cb_ref[...] + (cw_ref[0:1, :] * hp[0:m, :] + cw_ref[1:2, :] * hp[bsz:bsz + m, :]
                              + cw_ref[2:3, :] * hp[2 * bsz:2 * bsz + m, :])

    u = conv(wu_ref, cwu_ref, cbu_ref, bu_ref, tu_ref)
    g = conv(wg_ref, cwg_ref, cbg_ref, bg_ref, tg_ref)
    part = jnp.dot((_gelu(g) * u).astype(BF16), wd_ref[...], preferred_element_type=F32)

    @pl.when(j == 0)
    def _first():
        acc_scr[...] = part

    @pl.when(j > 0)
    def _rest():
        acc_scr[...] += part

    @pl.when(j == pl.num_programs(0) - 1)
    def _fin():
        o_ref[...] = _layer_norm(DEEPNORM_ALPHA * x_ref[...] + acc_scr[...], lnw_ref[...], lnb_ref[...])


def _ffn_sample(x_pm, buf_pm, w_up, conv_w, conv_b, w_down, lnw, lnb, n_new, bsz):
    m = n_new * bsz
    nbuf = (CONV_WIDTH - 1) * bsz
    tf = FFN_TF
    nf = FFN_DIM // tf
    conv_b = conv_b.reshape(1, -1)
    ucol = lambda r: pl.BlockSpec((r, tf), lambda j: (0, j))
    gcol = lambda r: pl.BlockSpec((r, tf), lambda j: (0, nf + j))
    return pl.pallas_call(
        functools.partial(_ffn_sample_kernel, n_new=n_new, bsz=bsz),
        grid=(nf,),
        in_specs=[_full((m, D_MODEL)), ucol(nbuf), gcol(nbuf), ucol(D_MODEL), gcol(D_MODEL),
                  ucol(CONV_WIDTH), gcol(CONV_WIDTH), ucol(1), gcol(1),
                  pl.BlockSpec((tf, D_MODEL), lambda j: (j, 0)), _full((1, D_MODEL)), _full((1, D_MODEL))],
        out_specs=[_full((m, D_MODEL)), ucol(nbuf), ucol(nbuf)],
        out_shape=[jax.ShapeDtypeStruct((m, D_MODEL), F32), jax.ShapeDtypeStruct((nbuf, FFN_DIM), F32),
                   jax.ShapeDtypeStruct((nbuf, FFN_DIM), F32)],
        scratch_shapes=[pltpu.VMEM((m, D_MODEL), F32)],
        compiler_params=_cparams("arbitrary"),
    )(x_pm, buf_pm, buf_pm, w_up, w_up, conv_w, conv_w, conv_b, conv_b, w_down,
      lnw.reshape(1, -1), lnb.reshape(1, -1))


def kernel(x_prompt, x_sample, cache_k, cache_v, state_wkv, state_shift, state_conv, page_table, rel_bias_table, w_in, mu_shift, decay_base, decay_up, iclr_base, iclr_up, gate_up, kk_scale, k_iclr_mix, bonus_rk, gn_w, gn_b, lam_q1, lam_k1, lam_q2, lam_k2, subln_w, w_branch_a, w_branch_b, w_out, ln1_w, ln1_b, w_up, conv_w, conv_b, w_down, ln2_w, ln2_b):
    assert w_in.shape[0] == DEPTH == 1
    l = 0
    lam_init = 0.8 - 0.6 * math.exp(-0.3 * l)
    bp, seq, _ = x_prompt.shape
    bs, n_new, _ = x_sample.shape
    page = cache_k.shape[2]
    assert page == LANES and page >= MAX_DISTANCE and ATT_T >= MAX_DISTANCE and n_new <= SAMPLE_T_PAD

    w_in_b = w_in[l].astype(BF16)
    wa, wb, wo = w_branch_a[l].astype(BF16), w_branch_b[l].astype(BF16), w_out[l].astype(BF16)
    w_up_b, w_down_b = w_up[l].astype(BF16), w_down[l].astype(BF16)
    tm_params = (mu_shift[l], decay_base[l], decay_up[l], iclr_base[l], iclr_up[l], gate_up[l], kk_scale[l],
                 k_iclr_mix[l], bonus_rk[l].reshape(-1), gn_w[l], gn_b[l])
    lam_params = [p[l].reshape(1, -1) for p in (lam_q1, lam_k1, lam_q2, lam_k2)]
    subln = subln_w[l].reshape(1, -1)

    mp = bp * seq
    xp2 = x_prompt.reshape(mp, D_MODEL)
    cols, q, k, v, gates = _proj(xp2, w_in_b)
    cols3 = cols.reshape(bp, seq, N_SHIFT)
    ya, wkv_p = _timemix(cols3, jnp.zeros((bp, N_SHIFT), F32),
                         jnp.zeros((bp, A_HEADS, A_HEAD_DIM, A_HEAD_DIM), F32), tm_params,
                         bb=bp, tc=LANES, n_steps=LANES)
    r3 = lambda a: a.reshape(bp, seq, B_WIDTH)
    ob = _attn_prompt(r3(q), r3(k), r3(v), _prompt_bias(rel_bias_table), lam_params, subln, lam_init)
    x1 = _merge(xp2, ya.reshape(mp, A_WIDTH), ob.reshape(mp, B_WIDTH), gates, wa, wb, wo, ln1_w[l], ln1_b[l])
    y_p, tu, tg = _ffn_prompt(x1.reshape(bp, seq, D_MODEL), w_up_b, conv_w[l], conv_b[l], w_down_b,
                              ln2_w[l], ln2_b[l])
    nbuf = CONV_WIDTH - 1
    k_prompt = k.reshape(1, bp, seq, B_HEADS, 2, B_HEAD_DIM)
    v_prompt = v.reshape(1, bp, seq, B_HEADS, B_V_DIM)
    shift_prompt = cols3[:, seq - 1][None]
    conv_prompt = jnp.concatenate([tu[:, -1, SUBLANES - nbuf:], tg[:, -1, SUBLANES - nbuf:]], axis=-1)[None]

    tp = SAMPLE_T_PAD
    xs8 = jnp.pad(x_sample, ((0, 0), (0, tp - n_new), (0, 0))).reshape(bs * tp, D_MODEL)
    cols, q, k, v, gates = _proj(xs8, w_in_b)
    cols3 = cols.reshape(bs, tp, N_SHIFT)
    ya, wkv_s = _timemix(cols3, state_shift[l], state_wkv[l], tm_params, bb=4, tc=tp, n_steps=n_new)
    r3 = lambda a: a.reshape(bs, tp, B_WIDTH)
    n_phys = cache_k.shape[1]
    cache_kt = jnp.transpose(cache_k[l], (0, 2, 3, 4, 1)).reshape(n_phys, B_WIDTH, page)
    cache_v2 = cache_v[l].reshape(n_phys, page * B_HEADS, B_V_DIM)
    ob = _attn_sample(r3(q), r3(k), r3(v), cache_kt, cache_v2, page_table,
                      _sample_bias(rel_bias_table, n_new, page), lam_params, subln, lam_init, n_new)
    x1 = _merge(xs8, ya.reshape(bs * tp, A_WIDTH), ob.reshape(bs * tp, B_WIDTH), gates, wa, wb, wo,
                ln1_w[l], ln1_b[l])
    x1_pm = jnp.swapaxes(x1.reshape(bs, tp, D_MODEL)[:, :n_new], 0, 1).reshape(n_new * bs, D_MODEL)
    buf_pm = jnp.swapaxes(state_conv[l], 0, 1).reshape(nbuf * bs, 2 * FFN_DIM)
    y_pm, tu, tg = _ffn_sample(x1_pm, buf_pm, w_up_b, conv_w[l], conv_b[l], w_down_b, ln2_w[l], ln2_b[l],
                               n_new, bs)
    y_s = jnp.swapaxes(y_pm.reshape(n_new, bs, D_MODEL), 0, 1)
    conv_sample = jnp.swapaxes(jnp.concatenate([tu, tg], axis=-1).reshape(nbuf, bs, 2 * FFN_DIM), 0, 1)[None]
    k_sample = k.reshape(bs, tp, B_HEADS, 2, B_HEAD_DIM)[:, :n_new][None]
    v_sample = v.reshape(bs, tp, B_HEADS, B_V_DIM)[:, :n_new][None]
    shift_sample = cols3[:, n_new - 1][None]

    return (y_p, y_s, k_prompt, v_prompt, wkv_p[None], shift_prompt, conv_prompt,
            k_sample, v_sample, wkv_s[None], shift_sample, conv_sample)
```

```python
import functools
import math

import jax
import jax.numpy as jnp
from jax import lax
from jax.experimental import pallas as pl
from jax.experimental.pallas import tpu as pltpu

F32 = jnp.float32
BF16 = jnp.bfloat16

D_MODEL = 1024
DEPTH = 1
A_WIDTH = D_MODEL // 2
A_HEAD_DIM = 64
A_HEADS = A_WIDTH // A_HEAD_DIM
LORA_DECAY = 64
LORA_ICLR = 64
LORA_GATE = 128
N_SHIFT = 3 * A_WIDTH + LORA_DECAY + LORA_ICLR + LORA_GATE
B_WIDTH = D_MODEL // 2
B_HEAD_DIM = 64
B_V_DIM = 2 * B_HEAD_DIM
B_HEADS = B_WIDTH // B_V_DIM
N_COLS = N_SHIFT + 3 * B_WIDTH + 2 * D_MODEL
NUM_BUCKETS = 32
MAX_DISTANCE = 128
FFN_DIM = ((8 * D_MODEL // 3 + 255) // 256) * 256
CONV_WIDTH = 3
LN_EPS = 1e-5
GN_EPS = 64e-5
RMS_EPS = 1e-5
DEEPNORM_ALPHA = (2 * DEPTH) ** 0.25

LANES = 128
SUBLANES = 8
VMEM_LIMIT = 56 * 1024 * 1024

NEG_BIG = -1e30
LOG2E = math.log2(math.e)
HEAD_PAIRS = A_HEADS // 2

PROJ_TM = 256
TM_CHUNK = 64
ATT_T = 256
ATT_HEADS = 4
PAGES_PER_STEP = 32
SAMPLE_T_PAD = SUBLANES
MERGE_TM = 512
FFN_TM = 512
FFN_TF = FFN_DIM // 2


def _cparams(*sem):
    return pltpu.CompilerParams(dimension_semantics=sem, vmem_limit_bytes=VMEM_LIMIT)


def _full(shape):
    n = len(shape)
    return pl.BlockSpec(shape, lambda *_: (0,) * n)


def _proj_kernel(x_ref, w_ref, cols_ref, q_ref, k_ref, v_ref, gates_ref, k16_ref, v16_ref):
    xb = x_ref[...].astype(BF16)

    def mm(lo, hi):
        return jnp.dot(xb, w_ref[:, lo:hi], preferred_element_type=F32)

    o = N_SHIFT
    cols_ref[...] = mm(0, o)
    q_ref[...] = mm(o, o + B_WIDTH) * (B_HEAD_DIM ** -0.5 * LOG2E)
    k = mm(o + B_WIDTH, o + 2 * B_WIDTH)
    v = mm(o + 2 * B_WIDTH, o + 3 * B_WIDTH)
    k_ref[...] = k
    v_ref[...] = v
    k16_ref[...] = k.astype(BF16)
    v16_ref[...] = v.astype(BF16)
    gates_ref[...] = mm(o + 3 * B_WIDTH, N_COLS)


def _proj(x2d, w_in_bf16):
    m = x2d.shape[0]
    tm = min(PROJ_TM, m)
    outs = ((N_SHIFT, F32), (B_WIDTH, F32), (B_WIDTH, F32), (B_WIDTH, F32), (2 * D_MODEL, F32),
            (B_WIDTH, BF16), (B_WIDTH, BF16))
    return pl.pallas_call(
        _proj_kernel,
        grid=(m // tm,),
        in_specs=[pl.BlockSpec((tm, D_MODEL), lambda i: (i, 0)), _full((D_MODEL, N_COLS))],
        out_specs=[pl.BlockSpec((tm, w), lambda i: (i, 0)) for w, _ in outs],
        out_shape=[jax.ShapeDtypeStruct((m, w), dt) for w, dt in outs],
        compiler_params=_cparams("arbitrary"),
    )(x2d, w_in_bf16)


def _pair_lo_mask():
    return lax.broadcasted_iota(jnp.int32, (1, LANES), 1) < A_HEAD_DIM


def _head_sum(x):
    lo = _pair_lo_mask()
    out = []
    for p in range(HEAD_PAIRS):
        xp = x[:, p * LANES:(p + 1) * LANES]
        s_lo = jnp.sum(jnp.where(lo, xp, 0.0), axis=-1, keepdims=True)
        s_hi = jnp.sum(jnp.where(lo, 0.0, xp), axis=-1, keepdims=True)
        out.append(jnp.where(lo, s_lo, s_hi))
    return jnp.concatenate(out, axis=-1)


def _softplus(z):
    return jnp.maximum(z, 0.0) + jnp.log1p(jnp.exp(-jnp.abs(z)))


def _sigmoid(z):
    return 1.0 / (1.0 + jnp.exp(-z))


def _tm_prep(cols, carry_scr, b, n_valid, prm):
    mu_ref, dbase_ref, dup_ref, ibase_ref, iup_ref, gup_ref, kks_ref, kim_ref, bonus_ref = prm
    aw = A_WIDTH
    rows = lax.broadcasted_iota(jnp.int32, (cols.shape[0], 1), 0)
    prev = jnp.where(rows == 0, carry_scr[b], pltpu.roll(cols, 1, axis=0))
    carry_scr[b] = cols[n_valid - 1:n_valid, :]
    h = cols + (prev - cols) * mu_ref[...]
    r, k, v = h[:, 0:aw], h[:, aw:2 * aw], h[:, 2 * aw:3 * aw]
    h_wa = h[:, 3 * aw:3 * aw + LANES]
    h_g = h[:, 3 * aw + LANES:3 * aw + 2 * LANES]
    z = dbase_ref[...] + jnp.dot(jnp.tanh(h_wa).astype(BF16), dup_ref[...], preferred_element_type=F32)
    w_raw = -_softplus(-z) - 0.5
    logw = -jnp.exp(w_raw)
    a = _sigmoid(ibase_ref[...] + jnp.dot(h_wa.astype(BF16), iup_ref[...], preferred_element_type=F32))
    g = jnp.dot(_sigmoid(h_g).astype(BF16), gup_ref[...], preferred_element_type=F32)
    kk = k * kks_ref[...]
    kk = kk / jnp.maximum(jnp.sqrt(_head_sum(kk * kk)), 1e-12)
    k = k * (1.0 + (a - 1.0) * kim_ref[...])
    bon = _head_sum(r * k * bonus_ref[...]) * v
    return r, logw, k, v, kk, kk * a, g, bon


def _tm_finish(y, bon, g, gnw_ref, gnb_ref):
    inv_n = 1.0 / A_HEAD_DIM
    y = y + bon
    mu = _head_sum(y) * inv_n
    d = y - mu
    var = _head_sum(d * d) * inv_n
    return (d * lax.rsqrt(var + GN_EPS) * gnw_ref[...] + gnb_ref[...]) * g


def _split_pair(x):
    lo = _pair_lo_mask()
    return jnp.concatenate([jnp.where(lo, x, 0.0), jnp.where(lo, 0.0, x)], axis=0)


def _mmb(a, b):
    return jnp.dot(a.astype(BF16), b.astype(BF16), preferred_element_type=F32)


def _mmb_nt(a, b):
    return _nt_dot(a.astype(BF16), b.astype(BF16))


def _timemix_chunk_kernel(cols_ref, mu_ref, dbase_ref, dup_ref, ibase_ref, iup_ref, gup_ref,
                          kks_ref, kim_ref, bonus_ref, gnw_ref, gnb_ref,
                          y_ref, wkv_ref, s_scr, carry_scr, *, bb):
    c = pl.program_id(1)
    cs_n = TM_CHUNK
    n2 = 2 * cs_n

    @pl.when(c == 0)
    def _init():
        s_scr[...] = jnp.zeros_like(s_scr)
        carry_scr[...] = jnp.zeros_like(carry_scr)

    prm = (mu_ref, dbase_ref, dup_ref, ibase_ref, iup_ref, gup_ref, kks_ref, kim_ref, bonus_ref)
    rr = lax.broadcasted_iota(jnp.int32, (n2, n2), 0)
    cc = lax.broadcasted_iota(jnp.int32, (n2, n2), 1)
    strict = rr > cc
    incl = rr >= cc
    eye = (rr == cc).astype(F32)
    tri = (lax.broadcasted_iota(jnp.int32, (cs_n, cs_n), 0)
           >= lax.broadcasted_iota(jnp.int32, (cs_n, cs_n), 1)).astype(BF16)

    pairs, tails = [], []
    for b in range(bb):
        r, logw, k, v, kk, ak, g, bon = _tm_prep(cols_ref[b], carry_scr, b, cs_n, prm)
        hi = logw.astype(BF16)
        r1 = logw - hi.astype(F32)
        mid = r1.astype(BF16)
        low = (r1 - mid.astype(F32)).astype(BF16)
        cum = (jnp.dot(tri, hi, preferred_element_type=F32) + jnp.dot(tri, mid, preferred_element_type=F32)
               + jnp.dot(tri, low, preferred_element_type=F32))
        e_prev, e_cur, e_inv = jnp.exp(cum - logw), jnp.exp(cum), jnp.exp(-cum)
        kt_all, rt_all, kb_all, ab_all = kk * e_prev, r * e_cur, k * e_inv, ak * e_inv
        tails.append((bon, g))
        for p in range(HEAD_PAIRS):
            sl = slice(p * LANES, (p + 1) * LANES)
            kt, rt, kb, ab, vv = (_split_pair(x[:, sl]) for x in (kt_all, rt_all, kb_all, ab_all, v))
            pairs.append(dict(kt=kt, rt=rt, kb=kb, ab=ab, vv=vv, decay=e_cur[cs_n - 1:cs_n, sl]))

    for q in pairs:
        gram = _mmb_nt(jnp.concatenate([q['kt'], q['rt']], axis=0),
                       jnp.concatenate([q['kb'], q['ab']], axis=0))
        q['akv'] = jnp.where(strict, gram[0:n2, 0:n2], 0.0)
        q['pw'] = jnp.where(strict, -gram[0:n2, n2:2 * n2], 0.0)
        q['brk'] = jnp.where(incl, gram[n2:2 * n2, 0:n2], 0.0)
        q['bra'] = jnp.where(incl, gram[n2:2 * n2, n2:2 * n2], 0.0)
        q['t'] = eye + q['pw']
    n = 1
    while 2 * n < cs_n:
        for q in pairs:
            q['pw'] = _mmb(q['pw'], q['pw'])
        for q in pairs:
            q['t'] = q['t'] + _mmb(q['t'], q['pw'])
        n *= 2
    for q in pairs:
        q['vt'] = q['vv'].T
        q['w1'] = _mmb(q['t'], q['kt'])
    for q in pairs:
        q['va'] = _mmb_nt(q['vt'], q['akv'])
    for i, q in enumerate(pairs):
        q['s'] = s_scr[i]
        q['ut'] = _mmb_nt(q['s'], q['w1']) + _mmb_nt(q['va'], q['t'])
    for i, q in enumerate(pairs):
        s_scr[i] = (q['s'] + _mmb(q['vt'], q['kb']) - _mmb(q['ut'], q['ab'])) * q['decay']
    for q in pairs:
        q['yt'] = _mmb_nt(q['s'], q['rt']) + _mmb_nt(q['vt'], q['brk']) - _mmb_nt(q['ut'], q['bra'])
    for b in range(bb):
        ys = []
        for p in range(HEAD_PAIRS):
            y2 = pairs[b * HEAD_PAIRS + p]['yt'].T
            ys.append(y2[0:cs_n, :] + y2[cs_n:n2, :])
        bon, g = tails[b]
        y_ref[b] = _tm_finish(jnp.concatenate(ys, axis=-1), bon, g, gnw_ref, gnb_ref)

    @pl.when(c == pl.num_programs(1) - 1)
    def _fin():
        wkv_ref[...] = s_scr[...].reshape(bb, HEAD_PAIRS, LANES, LANES)


def _timemix_chunk(cols, tm_params, *, bb):
    bsz, seq, _ = cols.shape
    tc = TM_CHUNK
    params = _tm_param_arrays(tm_params)
    y, wkv = pl.pallas_call(
        functools.partial(_timemix_chunk_kernel, bb=bb),
        grid=(bsz // bb, seq // tc),
        in_specs=[pl.BlockSpec((bb, tc, N_SHIFT), lambda i, c: (i, c, 0))] + [_full(p.shape) for p in params],
        out_specs=[pl.BlockSpec((bb, tc, A_WIDTH), lambda i, c: (i, c, 0)),
                   pl.BlockSpec((bb, HEAD_PAIRS, LANES, LANES), lambda i, c: (i, 0, 0, 0))],
        out_shape=[jax.ShapeDtypeStruct((bsz, seq, A_WIDTH), F32),
                   jax.ShapeDtypeStruct((bsz, HEAD_PAIRS, LANES, LANES), F32)],
        scratch_shapes=[pltpu.VMEM((bb * HEAD_PAIRS, LANES, LANES), F32), pltpu.VMEM((bb, 1, N_SHIFT), F32)],
        compiler_params=_cparams("arbitrary", "arbitrary"),
    )(cols, *params)
    wkv = wkv.reshape(bsz, HEAD_PAIRS, 2, A_HEAD_DIM, 2, A_HEAD_DIM)
    wkv = jnp.stack([wkv[:, :, 0, :, 0, :], wkv[:, :, 1, :, 1, :]], axis=2)
    return y, wkv.reshape(bsz, A_HEADS, A_HEAD_DIM, A_HEAD_DIM)


def _tm_param_arrays(tm_params):
    vec = lambda a: a.reshape(1, -1)
    mu, dbase, dup, ibase, iup, gup, kks, kim, bonus, gnw, gnb = tm_params
    zpad = jnp.zeros((LORA_DECAY, A_WIDTH), F32)
    dup_p = jnp.concatenate([dup, zpad], 0).astype(BF16)
    iup_p = jnp.concatenate([zpad, iup], 0).astype(BF16)
    return [vec(mu), vec(dbase), dup_p, vec(ibase), iup_p, gup.astype(BF16), vec(kks), vec(kim), vec(bonus),
            vec(gnw), vec(gnb)]


def _timemix_kernel(cols_ref, shift0_ref, wkv0_ref, mu_ref, dbase_ref, dup_ref, ibase_ref, iup_ref, gup_ref,
                    kks_ref, kim_ref, bonus_ref, gnw_ref, gnb_ref,
                    y_ref, wkv_ref,
                    s_scr, carry_scr, r_scr, w_scr, k_scr, kk_scr, ak_scr, g_scr, bon_scr, vpad_scr, vt_scr, yt_scr,
                    *, bb, tc, n_steps):
    c = pl.program_id(1)

    @pl.when(c == 0)
    def _init():
        s_scr[...] = wkv0_ref[...].reshape(bb * HEAD_PAIRS, A_HEAD_DIM, LANES)
        carry_scr[...] = shift0_ref[...]
        vpad_scr[...] = jnp.zeros_like(vpad_scr)

    prm = (mu_ref, dbase_ref, dup_ref, ibase_ref, iup_ref, gup_ref, kks_ref, kim_ref, bonus_ref)
    for b in range(bb):
        r, logw, k, v, kk, ak, g, bon = _tm_prep(cols_ref[b], carry_scr, b, n_steps, prm)
        for ref, val in ((r_scr, r), (w_scr, jnp.exp(logw)), (k_scr, k), (kk_scr, kk), (ak_scr, ak)):
            for p in range(HEAD_PAIRS):
                ref[b * HEAD_PAIRS + p] = val[:, p * LANES:(p + 1) * LANES]
        g_scr[b] = g
        bon_scr[b] = bon
        if tc == LANES:
            vt_scr[b] = v.T
        else:
            vpad_scr[b, 0:tc, :] = v
            vt_scr[b] = vpad_scr[b].T
    yt_scr[...] = jnp.zeros_like(yt_scr)

    lane = lax.broadcasted_iota(jnp.int32, (1, LANES), 1)
    lo = lane < A_HEAD_DIM
    hd = A_HEAD_DIM

    def seg_sum(x):
        s_lo = jnp.sum(jnp.where(lo, x, 0.0), axis=-1, keepdims=True)
        s_hi = jnp.sum(jnp.where(lo, 0.0, x), axis=-1, keepdims=True)
        return s_lo, s_hi

    n_sub = min(SUBLANES, n_steps)

    def block(i, carry):
        base = pl.multiple_of(i * SUBLANES, SUBLANES)
        for j in range(n_sub):
            tmask = lane == base + j
            for b in range(bb):
                for p in range(HEAD_PAIRS):
                    idx = b * HEAD_PAIRS + p
                    ra, rb = p * LANES, p * LANES + hd
                    r_t, w_t, k_t, kk_t, ak_t = (
                        jnp.broadcast_to(ref[idx, pl.ds(base, SUBLANES), :][j:j + 1, :], (hd, LANES))
                        for ref in (r_scr, w_scr, k_scr, kk_scr, ak_scr))
                    v_a = jnp.sum(jnp.where(tmask, vt_scr[b, ra:ra + hd, :], 0.0), axis=-1, keepdims=True)
                    v_b = jnp.sum(jnp.where(tmask, vt_scr[b, rb:rb + hd, :], 0.0), axis=-1, keepdims=True)
                    v_col = jnp.where(lo, v_a, v_b)
                    s = s_scr[idx]
                    u_a, u_b = seg_sum(s * kk_t)
                    s = s * w_t - jnp.where(lo, u_a, u_b) * ak_t + v_col * k_t
                    s_scr[idx] = s
                    y_a, y_b = seg_sum(s * r_t)
                    yt_scr[b, ra:ra + hd, :] = jnp.where(tmask, y_a, yt_scr[b, ra:ra + hd, :])
                    yt_scr[b, rb:rb + hd, :] = jnp.where(tmask, y_b, yt_scr[b, rb:rb + hd, :])
        return carry

    lax.fori_loop(0, n_steps // n_sub, block, 0)

    for b in range(bb):
        y_ref[b] = _tm_finish(yt_scr[b].T[0:tc, :], bon_scr[b], g_scr[b], gnw_ref, gnb_ref)

    @pl.when(c == pl.num_programs(1) - 1)
    def _fin():
        wkv_ref[...] = s_scr[...].reshape(bb, HEAD_PAIRS, A_HEAD_DIM, LANES)


def _pack_pairs(wkv):
    b = wkv.shape[0]
    x = wkv.reshape(b, HEAD_PAIRS, 2, A_HEAD_DIM, A_HEAD_DIM)
    return jnp.swapaxes(x, 2, 3).reshape(b, HEAD_PAIRS, A_HEAD_DIM, LANES)


def _unpack_pairs(x):
    b = x.shape[0]
    x = x.reshape(b, HEAD_PAIRS, A_HEAD_DIM, 2, A_HEAD_DIM)
    return jnp.swapaxes(x, 2, 3).reshape(b, A_HEADS, A_HEAD_DIM, A_HEAD_DIM)


def _timemix(cols, shift0, wkv0, tm_params, *, bb, tc, n_steps):
    bsz, seq, _ = cols.shape
    nb, nc = bsz // bb, seq // tc
    params = _tm_param_arrays(tm_params)
    scr = lambda *s: pltpu.VMEM(s, F32)
    y, wkv = pl.pallas_call(
        functools.partial(_timemix_kernel, bb=bb, tc=tc, n_steps=n_steps),
        grid=(nb, nc),
        in_specs=[pl.BlockSpec((bb, tc, N_SHIFT), lambda i, c: (i, c, 0)),
                  pl.BlockSpec((bb, 1, N_SHIFT), lambda i, c: (i, 0, 0)),
                  pl.BlockSpec((bb, HEAD_PAIRS, A_HEAD_DIM, LANES), lambda i, c: (i, 0, 0, 0))]
                 + [_full(p.shape) for p in params],
        out_specs=[pl.BlockSpec((bb, tc, A_WIDTH), lambda i, c: (i, c, 0)),
                   pl.BlockSpec((bb, HEAD_PAIRS, A_HEAD_DIM, LANES), lambda i, c: (i, 0, 0, 0))],
        out_shape=[jax.ShapeDtypeStruct((bsz, seq, A_WIDTH), F32),
                   jax.ShapeDtypeStruct((bsz, HEAD_PAIRS, A_HEAD_DIM, LANES), F32)],
        scratch_shapes=[scr(bb * HEAD_PAIRS, A_HEAD_DIM, LANES), scr(bb, 1, N_SHIFT)]
                       + [scr(bb * HEAD_PAIRS, tc, LANES)] * 5 + [scr(bb, tc, A_WIDTH)] * 2
                       + [scr(bb, LANES, A_WIDTH), scr(bb, A_WIDTH, LANES), scr(bb, A_WIDTH, LANES)],
        compiler_params=_cparams("arbitrary", "arbitrary"),
    )(cols, shift0.reshape(bsz, 1, N_SHIFT), _pack_pairs(wkv0), *params)
    return y, _unpack_pairs(wkv)


def _bucket(n):
    max_exact = NUM_BUCKETS // 2
    nf = jnp.maximum(n, 1).astype(F32)
    large = max_exact + jnp.floor(jnp.log(nf / max_exact) / math.log(MAX_DISTANCE / max_exact)
                                  * (NUM_BUCKETS - max_exact)).astype(jnp.int32)
    return jnp.where(n < max_exact, n, jnp.minimum(large, NUM_BUCKETS - 1))


def _prompt_bias_kernel(table_ref, out_ref):
    h = pl.program_id(0)
    t = ATT_T
    rr = lax.broadcasted_iota(jnp.int32, (t, t), 0)
    cc = lax.broadcasted_iota(jnp.int32, (t, t), 1)
    far = table_ref[NUM_BUCKETS - 1, h]
    for i in range(2):
        d = (1 - i) * t + rr - cc
        bkt = _bucket(jnp.maximum(d, 0))
        bias = jnp.zeros((t, t), F32)
        for bk in range(NUM_BUCKETS):
            bias = jnp.where(bkt == bk, table_ref[bk, h], bias)
        out_ref[0, i] = jnp.where(d >= 0, (bias - far) * LOG2E, NEG_BIG)


def _prompt_bias(table):
    t = ATT_T
    return pl.pallas_call(
        _prompt_bias_kernel,
        grid=(B_HEADS,),
        in_specs=[pl.BlockSpec(memory_space=pltpu.SMEM)],
        out_specs=pl.BlockSpec((1, 2, t, t), lambda h: (h, 0, 0, 0)),
        out_shape=jax.ShapeDtypeStruct((B_HEADS, 2, t, t), F32),
        compiler_params=_cparams("arbitrary"),
    )(table)


def _sample_rows(n_new):
    return n_new * 2 * B_HEADS


def _sample_bias_kernel(table_ref, out_ref, *, n_new, page):
    nr = _sample_rows(n_new)
    rr = lax.broadcasted_iota(jnp.int32, (nr, LANES), 0)
    cc = lax.broadcasted_iota(jnp.int32, (nr, LANES), 1)
    tok = rr % n_new
    head = rr // (2 * n_new)

    def lookup(d):
        bkt = _bucket(jnp.maximum(d, 0))
        bias = jnp.zeros((nr, LANES), F32)
        for hh in range(B_HEADS):
            for bk in range(NUM_BUCKETS):
                bias = jnp.where((bkt == bk) & (head == hh), table_ref[bk, hh], bias)
        return bias

    far = lookup(jnp.full((nr, LANES), MAX_DISTANCE, jnp.int32))
    out_ref[0] = far * LOG2E
    out_ref[1] = lookup(tok + page - cc) * LOG2E
    d_self = tok - cc
    out_ref[2] = jnp.where((d_self >= 0) & (cc < n_new), lookup(d_self) * LOG2E, NEG_BIG)


def _sample_bias(table, n_new, page):
    nr = _sample_rows(n_new)
    return pl.pallas_call(
        functools.partial(_sample_bias_kernel, n_new=n_new, page=page),
        in_specs=[pl.BlockSpec(memory_space=pltpu.SMEM)],
        out_specs=pl.BlockSpec(memory_space=pltpu.VMEM),
        out_shape=jax.ShapeDtypeStruct((3, nr, LANES), F32),
    )(table)


def _lambda(lq1_ref, lk1_ref, lq2_ref, lk2_ref, lam_init):
    s1 = jnp.sum(lq1_ref[...] * lk1_ref[...], axis=-1, keepdims=True)
    s2 = jnp.sum(lq2_ref[...] * lk2_ref[...], axis=-1, keepdims=True)
    return jnp.exp(s1) - jnp.exp(s2) + lam_init


def _nt_dot(a, b):
    return lax.dot_general(a, b, (((1,), (1,)), ((), ())), preferred_element_type=F32)


def _attn_prompt_kernel(q_ref, k_ref, v_ref, bias_ref, lq1_ref, lk1_ref, lq2_ref, lk2_ref, subln_ref, o_ref,
                        acc_scr, *, lam_init):
    i = pl.program_id(2)
    t = ATT_T
    nh = ATT_HEADS
    lane = lax.broadcasted_iota(jnp.int32, (1, LANES), 1)
    first = lane < B_HEAD_DIM
    chains = [(hh, c) for hh in range(nh) for c in range(2)]
    qs = []
    for hh, c in chains:
        q = q_ref[0, :, hh * B_V_DIM:(hh + 1) * B_V_DIM]
        qs.append((jnp.where(first, q, 0.0) if c == 0 else jnp.where(first, 0.0, q)).astype(BF16))
    acc_scr[...] = jnp.zeros_like(acc_scr)
    ones = jnp.ones((t, B_V_DIM), BF16)

    def tile(j, carry, near):
        ms, ls = carry
        off = pl.multiple_of(j * t, t)
        kts = [k_ref[0, pl.ds(off, t), hh * B_V_DIM:(hh + 1) * B_V_DIM] for hh in range(nh)]
        vts = [v_ref[0, pl.ds(off, t), hh * B_V_DIM:(hh + 1) * B_V_DIM] for hh in range(nh)]
        ss = [_nt_dot(qs[n], kts[hh]) for n, (hh, _) in enumerate(chains)]
        if near:
            biases = [bias_ref[hh, j - i + 1] for hh in range(nh)]
            ss = [s + biases[hh] for s, (hh, _) in zip(ss, chains)]
        m_new = [jnp.maximum(m, jnp.max(s, axis=-1, keepdims=True)) for m, s in zip(ms, ss)]
        ps = [jnp.exp2(s - m).astype(BF16) for s, m in zip(ss, m_new)]
        alphas = [jnp.exp2(m - mn) for m, mn in zip(ms, m_new)]
        ls = tuple(a * l + jnp.dot(p, ones, preferred_element_type=F32) for a, l, p in zip(alphas, ls, ps))
        for n, (hh, _) in enumerate(chains):
            acc_scr[n] = alphas[n] * acc_scr[n] + jnp.dot(ps[n], vts[hh], preferred_element_type=F32)
        return tuple(m_new), ls

    m0 = tuple(jnp.full((t, 1), NEG_BIG, F32) for _ in chains)
    l0 = tuple(jnp.zeros((t, B_V_DIM), F32) for _ in chains)
    n_far = jnp.maximum(i - 1, 0)
    carry = lax.fori_loop(0, n_far, functools.partial(tile, near=False), (m0, l0))
    _, ls = lax.fori_loop(n_far, i + 1, functools.partial(tile, near=True), carry)
    lam = _lambda(lq1_ref, lk1_ref, lq2_ref, lk2_ref, lam_init)
    for hh in range(nh):
        o = acc_scr[2 * hh] * (1.0 / ls[2 * hh]) - acc_scr[2 * hh + 1] * (lam / ls[2 * hh + 1])
        ms = jnp.sum(o * o, axis=-1, keepdims=True) * (1.0 / B_V_DIM)
        o_ref[0, :, hh * B_V_DIM:(hh + 1) * B_V_DIM] = (o * lax.rsqrt(ms + RMS_EPS) * subln_ref[...]
                                                        * (1.0 - lam_init))


def _attn_prompt(q, k16, v16, bias, lam_params, subln, lam_init):
    bsz, seq, _ = q.shape
    t = ATT_T
    nh = ATT_HEADS
    w = nh * B_V_DIM
    lam_specs = [_full((1, B_HEAD_DIM))] * 4
    return pl.pallas_call(
        functools.partial(_attn_prompt_kernel, lam_init=lam_init),
        grid=(bsz, B_HEADS // nh, seq // t),
        in_specs=[pl.BlockSpec((1, t, w), lambda b, h, i: (b, i, h)),
                  pl.BlockSpec((1, seq, w), lambda b, h, i: (b, 0, h)),
                  pl.BlockSpec((1, seq, w), lambda b, h, i: (b, 0, h)),
                  pl.BlockSpec((nh, 2, t, t), lambda b, h, i: (h, 0, 0, 0))]
                 + lam_specs + [_full((1, B_V_DIM))],
        out_specs=pl.BlockSpec((1, t, w), lambda b, h, i: (b, i, h)),
        out_shape=jax.ShapeDtypeStruct((bsz, seq, B_WIDTH), F32),
        scratch_shapes=[pltpu.VMEM((2 * nh, t, B_V_DIM), F32)],
        compiler_params=_cparams("arbitrary", "arbitrary", "arbitrary"),
    )(q, k16, v16, bias, *lam_params, subln)


def _attn_sample_kernel(pt_ref, q_ref, kn_ref, vn_ref, sb_ref, lq1_ref, lk1_ref, lq2_ref, lk2_ref, subln_ref,
                        *rest, n_new, lam_init):
    pg = PAGES_PER_STEP
    k_refs, v_refs = rest[:pg], rest[pg:2 * pg]
    o_ref = rest[2 * pg]
    qb_scr, m_scr, l_scr, acc_scr, kn_scr, vn_scr = rest[2 * pg + 1:]
    del pt_ref
    g = pl.program_id(1)
    ng = pl.num_programs(1)
    nr = _sample_rows(n_new)
    rph = 2 * n_new
    page = kn_scr.shape[0]

    @pl.when(g == 0)
    def _init():
        rr = lax.broadcasted_iota(jnp.int32, (rph, B_WIDTH), 0)
        cc = lax.broadcasted_iota(jnp.int32, (rph, B_WIDTH), 1)
        q = jnp.where(rr < n_new, q_ref[0], 0.0)
        q2 = q + pltpu.roll(q, n_new, axis=0)
        for h in range(B_HEADS):
            sel = (cc // B_HEAD_DIM) == 2 * h + rr // n_new
            qb_scr[h * rph:(h + 1) * rph, :] = jnp.where(sel, q2, 0.0)
        m_scr[...] = jnp.full_like(m_scr, NEG_BIG)
        l_scr[...] = jnp.zeros_like(l_scr)
        acc_scr[...] = jnp.zeros_like(acc_scr)
        kn_scr[...] = jnp.zeros_like(kn_scr)
        vn_scr[...] = jnp.zeros_like(vn_scr)
        kn_scr[0:SAMPLE_T_PAD, :] = kn_ref[0]
        vn_scr[0:SAMPLE_T_PAD, :] = vn_ref[0]

    def update(s_list, v_list):
        m = m_scr[...]
        m_new = m
        for s in s_list:
            m_new = jnp.maximum(m_new, jnp.max(s, axis=-1, keepdims=True))
        alpha = jnp.exp2(m - m_new)
        l = alpha * l_scr[...]
        acc = [alpha[h * rph:(h + 1) * rph] * acc_scr[h * rph:(h + 1) * rph, :] for h in range(B_HEADS)]
        for s, v_heads in zip(s_list, v_list):
            p = jnp.exp2(s - m_new)
            l = l + jnp.sum(p, axis=-1, keepdims=True)
            for h in range(B_HEADS):
                acc[h] = acc[h] + jnp.dot(p[h * rph:(h + 1) * rph, :].astype(BF16), v_heads[h],
                                          preferred_element_type=F32)
        m_scr[...] = m_new
        l_scr[...] = l
        for h in range(B_HEADS):
            acc_scr[h * rph:(h + 1) * rph, :] = acc[h]

    qb = qb_scr[...].astype(BF16)
    far = sb_ref[0]
    s_list, v_list = [], []
    for i in range(pg):
        bias = far
        if i == pg - 1:
            bias = sb_ref[jnp.where(g == ng - 1, 1, 0)]
        s_list.append(jnp.dot(qb, k_refs[i][0].astype(BF16), preferred_element_type=F32) + bias)
        v_list.append([v_refs[i][0, pl.ds(h, page, stride=B_HEADS), :].astype(BF16) for h in range(B_HEADS)])
    update(s_list, v_list)

    @pl.when(g == ng - 1)
    def _fin():
        vn = vn_scr[...].astype(BF16)
        update([_nt_dot(qb, kn_scr[...].astype(BF16)) + sb_ref[2]],
               [[vn[:, h * B_V_DIM:(h + 1) * B_V_DIM] for h in range(B_HEADS)]])
        lam = _lambda(lq1_ref, lk1_ref, lq2_ref, lk2_ref, lam_init)
        rr = lax.broadcasted_iota(jnp.int32, (nr, 1), 0)
        coef = jnp.where(rr % rph < n_new, 1.0, -lam) / l_scr[...]
        w = acc_scr[...] * coef
        parts = []
        for h in range(B_HEADS):
            wh = w[h * rph:(h + 1) * rph, :]
            oh = wh + pltpu.roll(wh, n_new, axis=0)
            ms = jnp.sum(oh * oh, axis=-1, keepdims=True) * (1.0 / B_V_DIM)
            parts.append(oh * lax.rsqrt(ms + RMS_EPS) * subln_ref[...] * (1.0 - lam_init))
        o_ref[0] = jnp.concatenate(parts, axis=-1)


def _attn_sample(q, k_new, v_new, cache_k, cache_v, page_table, sbias, lam_params, subln, lam_init, n_new):
    bsz = q.shape[0]
    n_pages = page_table.shape[1]
    page = cache_k.shape[2]
    pg = PAGES_PER_STEP
    nr = _sample_rows(n_new)
    assert 2 * n_new == SAMPLE_T_PAD == SUBLANES
    tok_spec = pl.BlockSpec((1, SAMPLE_T_PAD, B_WIDTH), lambda b, g, pt: (b, 0, 0))

    def page_spec(i):
        return pl.BlockSpec((1, B_WIDTH, page), lambda b, g, pt: (pt[b, g * pg + i], 0, 0))

    const = lambda shape: pl.BlockSpec(shape, lambda b, g, pt: (0,) * len(shape))
    grid_spec = pltpu.PrefetchScalarGridSpec(
        num_scalar_prefetch=1,
        grid=(bsz, n_pages // pg),
        in_specs=[tok_spec, tok_spec, tok_spec, const((3, nr, LANES))]
                 + [const((1, B_HEAD_DIM))] * 4 + [const((1, B_V_DIM))]
                 + [page_spec(i) for i in range(pg)] * 2,
        out_specs=tok_spec,
        scratch_shapes=[pltpu.VMEM((nr, B_WIDTH), F32), pltpu.VMEM((nr, 1), F32), pltpu.VMEM((nr, 1), F32),
                        pltpu.VMEM((nr, B_V_DIM), F32), pltpu.VMEM((page, B_WIDTH), F32),
                        pltpu.VMEM((page, B_WIDTH), F32)],
    )
    return pl.pallas_call(
        functools.partial(_attn_sample_kernel, n_new=n_new, lam_init=lam_init),
        grid_spec=grid_spec,
        out_shape=jax.ShapeDtypeStruct((bsz, SAMPLE_T_PAD, B_WIDTH), F32),
        compiler_params=_cparams("arbitrary", "arbitrary"),
    )(page_table, q, k_new, v_new, sbias, *lam_params, subln, *([cache_k] * pg), *([cache_v] * pg))


def _layer_norm(z, w, b):
    inv_d = 1.0 / D_MODEL
    mu = jnp.sum(z, axis=-1, keepdims=True) * inv_d
    d = z - mu
    var = jnp.sum(d * d, axis=-1, keepdims=True) * inv_d
    return d * lax.rsqrt(var + LN_EPS) * w + b


def _merge_kernel(x_ref, ya_ref, ob_ref, gates_ref, wa_ref, wb_ref, wo_ref, lnw_ref, lnb_ref, o_ref):
    ga = _sigmoid(gates_ref[:, 0:D_MODEL])
    gb = _sigmoid(gates_ref[:, D_MODEL:2 * D_MODEL])
    merged = (ga * jnp.dot(ya_ref[...].astype(BF16), wa_ref[...], preferred_element_type=F32)
              + gb * jnp.dot(ob_ref[...].astype(BF16), wb_ref[...], preferred_element_type=F32))
    mix = jnp.dot(merged.astype(BF16), wo_ref[...], preferred_element_type=F32)
    o_ref[...] = _layer_norm(DEEPNORM_ALPHA * x_ref[...] + mix, lnw_ref[...], lnb_ref[...])


def _merge(x2d, ya, ob, gates, wa, wb, wo, lnw, lnb):
    m = x2d.shape[0]
    tm = min(MERGE_TM, m)
    row = lambda w: pl.BlockSpec((tm, w), lambda i: (i, 0))
    return pl.pallas_call(
        _merge_kernel,
        grid=(m // tm,),
        in_specs=[row(D_MODEL), row(A_WIDTH), row(B_WIDTH), row(2 * D_MODEL),
                  _full(wa.shape), _full(wb.shape), _full(wo.shape), _full((1, D_MODEL)), _full((1, D_MODEL))],
        out_specs=row(D_MODEL),
        out_shape=jax.ShapeDtypeStruct((m, D_MODEL), F32),
        compiler_params=_cparams("arbitrary"),
    )(x2d, ya, ob, gates, wa, wb, wo, lnw.reshape(1, -1), lnb.reshape(1, -1))


def _gelu(x):
    return 0.5 * x * (1.0 + lax.erf(x * (2.0 ** -0.5)))


def _ffn_prompt_kernel(x_ref, wu_ref, wg_ref, cwu_ref, cwg_ref, cbu_ref, cbg_ref, wd_ref, lnw_ref, lnb_ref,
                       o_ref, tu_ref, tg_ref, acc_scr, cu_scr, cg_scr):
    i, j = pl.program_id(1), pl.program_id(2)
    tm = x_ref.shape[1]
    rows = lax.broadcasted_iota(jnp.int32, (tm, 1), 0)
    xb = x_ref[0].astype(BF16)

    @pl.when(i == 0)
    def _reset():
        cu_scr[j] = jnp.zeros((SUBLANES, FFN_TF), F32)
        cg_scr[j] = jnp.zeros((SUBLANES, FFN_TF), F32)

    def conv(w_ref, cw_ref, cb_ref, carry_scr, tail_ref):
        h = jnp.dot(xb, w_ref[...], preferred_element_type=F32)
        carry = carry_scr[j]
        c6, c7 = carry[SUBLANES - 2:SUBLANES - 1, :], carry[SUBLANES - 1:SUBLANES, :]
        h1 = jnp.where(rows == 0, c7, pltpu.roll(h, 1, axis=0))
        h2 = jnp.where(rows == 0, c6, jnp.where(rows == 1, c7, pltpu.roll(h, 2, axis=0)))
        tail = h[tm - SUBLANES:tm, :]
        carry_scr[j] = tail
        tail_ref[0, 0] = tail
        return cb_ref[...] + (cw_ref[0:1, :] * h2 + cw_ref[1:2, :] * h1 + cw_ref[2:3, :] * h)

    u = conv(wu_ref, cwu_ref, cbu_ref, cu_scr, tu_ref)
    g = conv(wg_ref, cwg_ref, cbg_ref, cg_scr, tg_ref)
    part = jnp.dot((_gelu(g) * u).astype(BF16), wd_ref[...], preferred_element_type=F32)

    @pl.when(j == 0)
    def _first():
        acc_scr[...] = part

    @pl.when(j > 0)
    def _rest():
        acc_scr[...] += part

    @pl.when(j == pl.num_programs(2) - 1)
    def _fin():
        o_ref[0] = _layer_norm(DEEPNORM_ALPHA * x_ref[0] + acc_scr[...], lnw_ref[...], lnb_ref[...])


def _ffn_prompt(x, w_up, conv_w, conv_b, w_down, lnw, lnb):
    bsz, seq, _ = x.shape
    tm, tf = FFN_TM, FFN_TF
    nf = FFN_DIM // tf
    nt = seq // tm
    conv_b = conv_b.reshape(1, -1)
    ucol = lambda r: pl.BlockSpec((r, tf), lambda b, i, j: (0, j))
    gcol = lambda r: pl.BlockSpec((r, tf), lambda b, i, j: (0, nf + j))
    tail = pl.BlockSpec((1, 1, SUBLANES, tf), lambda b, i, j: (b, i, 0, j))
    return pl.pallas_call(
        _ffn_prompt_kernel,
        grid=(bsz, nt, nf),
        in_specs=[pl.BlockSpec((1, tm, D_MODEL), lambda b, i, j: (b, i, 0)),
                  ucol(D_MODEL), gcol(D_MODEL), ucol(CONV_WIDTH), gcol(CONV_WIDTH), ucol(1), gcol(1),
                  pl.BlockSpec((tf, D_MODEL), lambda b, i, j: (j, 0)),
                  pl.BlockSpec((1, D_MODEL), lambda b, i, j: (0, 0)),
                  pl.BlockSpec((1, D_MODEL), lambda b, i, j: (0, 0))],
        out_specs=[pl.BlockSpec((1, tm, D_MODEL), lambda b, i, j: (b, i, 0)), tail, tail],
        out_shape=[jax.ShapeDtypeStruct((bsz, seq, D_MODEL), F32),
                   jax.ShapeDtypeStruct((bsz, nt, SUBLANES, FFN_DIM), F32),
                   jax.ShapeDtypeStruct((bsz, nt, SUBLANES, FFN_DIM), F32)],
        scratch_shapes=[pltpu.VMEM((tm, D_MODEL), F32), pltpu.VMEM((nf, SUBLANES, tf), F32),
                        pltpu.VMEM((nf, SUBLANES, tf), F32)],
        compiler_params=_cparams("arbitrary", "arbitrary", "arbitrary"),
    )(x, w_up, w_up, conv_w, conv_w, conv_b, conv_b, w_down, lnw.reshape(1, -1), lnb.reshape(1, -1))


def _ffn_sample_kernel(x_ref, bu_ref, bg_ref, wu_ref, wg_ref, cwu_ref, cwg_ref, cbu_ref, cbg_ref, wd_ref,
                       lnw_ref, lnb_ref, o_ref, tu_ref, tg_ref, acc_scr, *, n_new, bsz):
    j = pl.program_id(0)
    m = n_new * bsz
    nbuf = (CONV_WIDTH - 1) * bsz
    xb = x_ref[...].astype(BF16)

    def conv(w_ref, cw_ref, cb_ref, buf_ref, tail_ref):
        h = jnp.dot(xb, w_ref[...], preferred_element_type=F32)
        hp = jnp.concatenate([buf_ref[...], h], axis=0)
        tail_ref[...] = hp[m:m + nbuf, :]
        return cb_ref[...] + (cw_ref[0:1, :] * hp[0:m, :] + cw_ref[1:2, :] * hp[bsz:bsz + m, :]
                              + cw_ref[2:3, :] * hp[2 * bsz:2 * bsz + m, :])

    u = conv(wu_ref, cwu_ref, cbu_ref, bu_ref, tu_ref)
    g = conv(wg_ref, cwg_ref, cbg_ref, bg_ref, tg_ref)
    part = jnp.dot((_gelu(g) * u).astype(BF16), wd_ref[...], preferred_element_type=F32)

    @pl.when(j == 0)
    def _first():
        acc_scr[...] = part

    @pl.when(j > 0)
    def _rest():
        acc_scr[...] += part

    @pl.when(j == pl.num_programs(0) - 1)
    def _fin():
        o_ref[...] = _layer_norm(DEEPNORM_ALPHA * x_ref[...] + acc_scr[...], lnw_ref[...], lnb_ref[...])


def _ffn_sample(x_pm, buf_pm, w_up, conv_w, conv_b, w_down, lnw, lnb, n_new, bsz):
    m = n_new * bsz
    nbuf = (CONV_WIDTH - 1) * bsz
    tf = FFN_TF
    nf = FFN_DIM // tf
    conv_b = conv_b.reshape(1, -1)
    ucol = lambda r: pl.BlockSpec((r, tf), lambda j: (0, j))
    gcol = lambda r: pl.BlockSpec((r, tf), lambda j: (0, nf + j))
    return pl.pallas_call(
        functools.partial(_ffn_sample_kernel, n_new=n_new, bsz=bsz),
        grid=(nf,),
        in_specs=[_full((m, D_MODEL)), ucol(nbuf), gcol(nbuf), ucol(D_MODEL), gcol(D_MODEL),
                  ucol(CONV_WIDTH), gcol(CONV_WIDTH), ucol(1), gcol(1),
                  pl.BlockSpec((tf, D_MODEL), lambda j: (j, 0)), _full((1, D_MODEL)), _full((1, D_MODEL))],
        out_specs=[_full((m, D_MODEL)), ucol(nbuf), ucol(nbuf)],
        out_shape=[jax.ShapeDtypeStruct((m, D_MODEL), F32), jax.ShapeDtypeStruct((nbuf, FFN_DIM), F32),
                   jax.ShapeDtypeStruct((nbuf, FFN_DIM), F32)],
        scratch_shapes=[pltpu.VMEM((m, D_MODEL), F32)],
        compiler_params=_cparams("arbitrary"),
    )(x_pm, buf_pm, buf_pm, w_up, w_up, conv_w, conv_w, conv_b, conv_b, w_down,
      lnw.reshape(1, -1), lnb.reshape(1, -1))


def kernel(x_prompt, x_sample, cache_k, cache_v, state_wkv, state_shift, state_conv, page_table, rel_bias_table, w_in, mu_shift, decay_base, decay_up, iclr_base, iclr_up, gate_up, kk_scale, k_iclr_mix, bonus_rk, gn_w, gn_b, lam_q1, lam_k1, lam_q2, lam_k2, subln_w, w_branch_a, w_branch_b, w_out, ln1_w, ln1_b, w_up, conv_w, conv_b, w_down, ln2_w, ln2_b):
    assert w_in.shape[0] == DEPTH == 1
    l = 0
    lam_init = 0.8 - 0.6 * math.exp(-0.3 * l)
    bp, seq, _ = x_prompt.shape
    bs, n_new, _ = x_sample.shape
    page = cache_k.shape[2]
    assert page == LANES and page >= MAX_DISTANCE and ATT_T >= MAX_DISTANCE and n_new <= SAMPLE_T_PAD

    w_in_b = w_in[l].astype(BF16)
    wa, wb, wo = w_branch_a[l].astype(BF16), w_branch_b[l].astype(BF16), w_out[l].astype(BF16)
    w_up_b, w_down_b = w_up[l].astype(BF16), w_down[l].astype(BF16)
    tm_params = (mu_shift[l], decay_base[l], decay_up[l], iclr_base[l], iclr_up[l], gate_up[l], kk_scale[l],
                 k_iclr_mix[l], bonus_rk[l].reshape(-1), gn_w[l], gn_b[l])
    lam_params = [p[l].reshape(1, -1) for p in (lam_q1, lam_k1, lam_q2, lam_k2)]
    subln = subln_w[l].reshape(1, -1)

    mp = bp * seq
    xp2 = x_prompt.reshape(mp, D_MODEL)
    cols, q, k, v, gates, k16, v16 = _proj(xp2, w_in_b)
    cols3 = cols.reshape(bp, seq, N_SHIFT)
    ya, wkv_p = _timemix_chunk(cols3, tm_params, bb=bp)
    r3 = lambda a: a.reshape(bp, seq, B_WIDTH)
    ob = _attn_prompt(r3(q), r3(k16), r3(v16), _prompt_bias(rel_bias_table), lam_params, subln, lam_init)
    x1 = _merge(xp2, ya.reshape(mp, A_WIDTH), ob.reshape(mp, B_WIDTH), gates, wa, wb, wo, ln1_w[l], ln1_b[l])
    y_p, tu, tg = _ffn_prompt(x1.reshape(bp, seq, D_MODEL), w_up_b, conv_w[l], conv_b[l], w_down_b,
                              ln2_w[l], ln2_b[l])
    nbuf = CONV_WIDTH - 1
    k_prompt = k.reshape(1, bp, seq, B_HEADS, 2, B_HEAD_DIM)
    v_prompt = v.reshape(1, bp, seq, B_HEADS, B_V_DIM)
    shift_prompt = cols3[:, seq - 1][None]
    conv_prompt = jnp.concatenate([tu[:, -1, SUBLANES - nbuf:], tg[:, -1, SUBLANES - nbuf:]], axis=-1)[None]

    tp = SAMPLE_T_PAD
    xs8 = jnp.pad(x_sample, ((0, 0), (0, tp - n_new), (0, 0))).reshape(bs * tp, D_MODEL)
    cols, q, k, v, gates, _, _ = _proj(xs8, w_in_b)
    cols3 = cols.reshape(bs, tp, N_SHIFT)
    ya, wkv_s = _timemix(cols3, state_shift[l], state_wkv[l], tm_params, bb=4, tc=tp, n_steps=n_new)
    r3 = lambda a: a.reshape(bs, tp, B_WIDTH)
    n_phys = cache_k.shape[1]
    cache_kt = jnp.transpose(cache_k[l], (0, 2, 3, 4, 1)).reshape(n_phys, B_WIDTH, page)
    cache_v2 = cache_v[l].reshape(n_phys, page * B_HEADS, B_V_DIM)
    ob = _attn_sample(r3(q), r3(k), r3(v), cache_kt, cache_v2, page_table,
                      _sample_bias(rel_bias_table, n_new, page), lam_params, subln, lam_init, n_new)
    x1 = _merge(xs8, ya.reshape(bs * tp, A_WIDTH), ob.reshape(bs * tp, B_WIDTH), gates, wa, wb, wo,
                ln1_w[l], ln1_b[l])
    x1_pm = jnp.swapaxes(x1.reshape(bs, tp, D_MODEL)[:, :n_new], 0, 1).reshape(n_new * bs, D_MODEL)
    buf_pm = jnp.swapaxes(state_conv[l], 0, 1).reshape(nbuf * bs, 2 * FFN_DIM)
    y_pm, tu, tg = _ffn_sample(x1_pm, buf_pm, w_up_b, conv_w[l], conv_b[l], w_down_b, ln2_w[l], ln2_b[l],
                               n_new, bs)
    y_s = jnp.swapaxes(y_pm.reshape(n_new, bs, D_MODEL), 0, 1)
    conv_sample = jnp.swapaxes(jnp.concatenate([tu, tg], axis=-1).reshape(nbuf, bs, 2 * FFN_DIM), 0, 1)[None]
    k_sample = k.reshape(bs, tp, B_HEADS, 2, B_HEAD_DIM)[:, :n_new][None]
    v_sample = v.reshape(bs, tp, B_HEADS, B_V_DIM)[:, :n_new][None]
    shift_sample = cols3[:, n_new - 1][None]

    return (y_p, y_s, k_prompt, v_prompt, wkv_p[None], shift_prompt, conv_prompt,
            k_sample, v_sample, wkv_s[None], shift_sample, conv_sample)
```

```python
import functools
import math

import jax
import jax.numpy as jnp
from jax import lax
from jax.experimental import pallas as pl
from jax.experimental.pallas import tpu as pltpu

F32 = jnp.float32
BF16 = jnp.bfloat16

D_MODEL = 1024
DEPTH = 1
A_WIDTH = D_MODEL // 2
A_HEAD_DIM = 64
A_HEADS = A_WIDTH // A_HEAD_DIM
LORA_DECAY = 64
LORA_ICLR = 64
LORA_GATE = 128
N_SHIFT = 3 * A_WIDTH + LORA_DECAY + LORA_ICLR + LORA_GATE
B_WIDTH = D_MODEL // 2
B_HEAD_DIM = 64
B_V_DIM = 2 * B_HEAD_DIM
B_HEADS = B_WIDTH // B_V_DIM
N_COLS = N_SHIFT + 3 * B_WIDTH + 2 * D_MODEL
NUM_BUCKETS = 32
MAX_DISTANCE = 128
FFN_DIM = ((8 * D_MODEL // 3 + 255) // 256) * 256
CONV_WIDTH = 3
LN_EPS = 1e-5
GN_EPS = 64e-5
RMS_EPS = 1e-5
DEEPNORM_ALPHA = (2 * DEPTH) ** 0.25

LANES = 128
SUBLANES = 8
VMEM_LIMIT = 56 * 1024 * 1024

NEG_BIG = -1e30
LOG2E = math.log2(math.e)
HEAD_PAIRS = A_HEADS // 2

PROJ_TM = 256
TM_CHUNK = 64
ATT_T = 512
ATT_HEADS = 4
PAGES_PER_STEP = 32
SAMPLE_T_PAD = SUBLANES
MERGE_TM = 512
FFN_TM = 512
FFN_TF = FFN_DIM // 2


def _cparams(*sem):
    return pltpu.CompilerParams(dimension_semantics=sem, vmem_limit_bytes=VMEM_LIMIT)


def _full(shape):
    n = len(shape)
    return pl.BlockSpec(shape, lambda *_: (0,) * n)


def _proj_kernel(x_ref, w_ref, cols_ref, q_ref, k_ref, v_ref, gates_ref, *mxu_refs, cache_shaped):
    xb = x_ref[...].astype(BF16)

    def mm(lo, hi):
        return jnp.dot(xb, w_ref[:, lo:hi], preferred_element_type=F32)

    o = N_SHIFT
    cols_ref[...] = mm(0, o)
    q_ref[...] = mm(o, o + B_WIDTH) * (B_HEAD_DIM ** -0.5 * LOG2E)
    k = mm(o + B_WIDTH, o + 2 * B_WIDTH)
    v = mm(o + 2 * B_WIDTH, o + 3 * B_WIDTH)
    if cache_shaped:
        for h in range(B_HEADS):
            v_ref[:, h, :] = v[:, h * B_V_DIM:(h + 1) * B_V_DIM]
            for c in range(2):
                lo = h * B_V_DIM + c * B_HEAD_DIM
                k_ref[:, h, c, :] = k[:, lo:lo + B_HEAD_DIM]
        k16_ref, v16_ref = mxu_refs
        k16_ref[...] = k.astype(BF16)
        v16_ref[...] = v.astype(BF16)
    else:
        k_ref[...] = k
        v_ref[...] = v
    gates_ref[...] = mm(o + 3 * B_WIDTH, N_COLS)


def _proj(x2d, w_in_bf16, cache_shaped):
    m = x2d.shape[0]
    tm = min(PROJ_TM, m)
    row = lambda w, dt: (pl.BlockSpec((tm, w), lambda i: (i, 0)), jax.ShapeDtypeStruct((m, w), dt))
    outs = [row(N_SHIFT, F32), row(B_WIDTH, F32)]
    if cache_shaped:
        outs += [(pl.BlockSpec((tm, B_HEADS, 2, B_HEAD_DIM), lambda i: (i, 0, 0, 0)),
                  jax.ShapeDtypeStruct((m, B_HEADS, 2, B_HEAD_DIM), F32)),
                 (pl.BlockSpec((tm, B_HEADS, B_V_DIM), lambda i: (i, 0, 0)),
                  jax.ShapeDtypeStruct((m, B_HEADS, B_V_DIM), F32)),
                 row(2 * D_MODEL, F32), row(B_WIDTH, BF16), row(B_WIDTH, BF16)]
    else:
        outs += [row(B_WIDTH, F32), row(B_WIDTH, F32), row(2 * D_MODEL, F32)]
    return pl.pallas_call(
        functools.partial(_proj_kernel, cache_shaped=cache_shaped),
        grid=(m // tm,),
        in_specs=[pl.BlockSpec((tm, D_MODEL), lambda i: (i, 0)), _full((D_MODEL, N_COLS))],
        out_specs=[s for s, _ in outs],
        out_shape=[t for _, t in outs],
        compiler_params=_cparams("arbitrary"),
    )(x2d, w_in_bf16)


def _pair_lo_mask():
    return lax.broadcasted_iota(jnp.int32, (1, LANES), 1) < A_HEAD_DIM


def _head_sum(x):
    lo = _pair_lo_mask()
    out = []
    for p in range(HEAD_PAIRS):
        xp = x[:, p * LANES:(p + 1) * LANES]
        s_lo = jnp.sum(jnp.where(lo, xp, 0.0), axis=-1, keepdims=True)
        s_hi = jnp.sum(jnp.where(lo, 0.0, xp), axis=-1, keepdims=True)
        out.append(jnp.where(lo, s_lo, s_hi))
    return jnp.concatenate(out, axis=-1)


def _softplus(z):
    return jnp.maximum(z, 0.0) + jnp.log1p(jnp.exp(-jnp.abs(z)))


def _sigmoid(z):
    return 1.0 / (1.0 + jnp.exp(-z))


def _tm_prep(cols, carry_scr, b, n_valid, prm):
    mu_ref, dbase_ref, dup_ref, ibase_ref, iup_ref, gup_ref, kks_ref, kim_ref, bonus_ref = prm
    aw = A_WIDTH
    rows = lax.broadcasted_iota(jnp.int32, (cols.shape[0], 1), 0)
    prev = jnp.where(rows == 0, carry_scr[b], pltpu.roll(cols, 1, axis=0))
    carry_scr[b] = cols[n_valid - 1:n_valid, :]
    h = cols + (prev - cols) * mu_ref[...]
    r, k, v = h[:, 0:aw], h[:, aw:2 * aw], h[:, 2 * aw:3 * aw]
    h_wa = h[:, 3 * aw:3 * aw + LANES]
    h_g = h[:, 3 * aw + LANES:3 * aw + 2 * LANES]
    z = dbase_ref[...] + jnp.dot(jnp.tanh(h_wa).astype(BF16), dup_ref[...], preferred_element_type=F32)
    w_raw = -_softplus(-z) - 0.5
    logw = -jnp.exp(w_raw)
    a = _sigmoid(ibase_ref[...] + jnp.dot(h_wa.astype(BF16), iup_ref[...], preferred_element_type=F32))
    g = jnp.dot(_sigmoid(h_g).astype(BF16), gup_ref[...], preferred_element_type=F32)
    kk = k * kks_ref[...]
    kk = kk / jnp.maximum(jnp.sqrt(_head_sum(kk * kk)), 1e-12)
    k = k * (1.0 + (a - 1.0) * kim_ref[...])
    bon = _head_sum(r * k * bonus_ref[...]) * v
    return r, logw, k, v, kk, kk * a, g, bon


def _tm_finish(y, bon, g, gnw_ref, gnb_ref):
    inv_n = 1.0 / A_HEAD_DIM
    y = y + bon
    mu = _head_sum(y) * inv_n
    d = y - mu
    var = _head_sum(d * d) * inv_n
    return (d * lax.rsqrt(var + GN_EPS) * gnw_ref[...] + gnb_ref[...]) * g


def _split_pair(x):
    lo = _pair_lo_mask()
    return jnp.concatenate([jnp.where(lo, x, 0.0), jnp.where(lo, 0.0, x)], axis=0)


def _mmb(a, b):
    return jnp.dot(a.astype(BF16), b.astype(BF16), preferred_element_type=F32)


def _mmb_nt(a, b):
    return _nt_dot(a.astype(BF16), b.astype(BF16))


def _timemix_chunk_kernel(cols_ref, mu_ref, dbase_ref, dup_ref, ibase_ref, iup_ref, gup_ref,
                          kks_ref, kim_ref, bonus_ref, gnw_ref, gnb_ref,
                          y_ref, wkv_ref, s_scr, carry_scr, *, bb):
    c = pl.program_id(1)
    cs_n = TM_CHUNK
    n2 = 2 * cs_n

    @pl.when(c == 0)
    def _init():
        s_scr[...] = jnp.zeros_like(s_scr)
        carry_scr[...] = jnp.zeros_like(carry_scr)

    prm = (mu_ref, dbase_ref, dup_ref, ibase_ref, iup_ref, gup_ref, kks_ref, kim_ref, bonus_ref)
    rr = lax.broadcasted_iota(jnp.int32, (n2, n2), 0)
    cc = lax.broadcasted_iota(jnp.int32, (n2, n2), 1)
    strict = rr > cc
    incl = rr >= cc
    eye = (rr == cc).astype(F32)
    tri = (lax.broadcasted_iota(jnp.int32, (cs_n, cs_n), 0)
           >= lax.broadcasted_iota(jnp.int32, (cs_n, cs_n), 1)).astype(BF16)

    pairs, tails = [], []
    for b in range(bb):
        r, logw, k, v, kk, ak, g, bon = _tm_prep(cols_ref[b], carry_scr, b, cs_n, prm)
        hi = logw.astype(BF16)
        r1 = logw - hi.astype(F32)
        mid = r1.astype(BF16)
        low = (r1 - mid.astype(F32)).astype(BF16)
        cum = (jnp.dot(tri, hi, preferred_element_type=F32) + jnp.dot(tri, mid, preferred_element_type=F32)
               + jnp.dot(tri, low, preferred_element_type=F32))
        e_prev, e_cur, e_inv = jnp.exp(cum - logw), jnp.exp(cum), jnp.exp(-cum)
        kt_all, rt_all, kb_all, ab_all = kk * e_prev, r * e_cur, k * e_inv, ak * e_inv
        tails.append((bon, g))
        for p in range(HEAD_PAIRS):
            sl = slice(p * LANES, (p + 1) * LANES)
            kt, rt, kb, ab, vv = (_split_pair(x[:, sl]) for x in (kt_all, rt_all, kb_all, ab_all, v))
            pairs.append(dict(kt=kt, rt=rt, kb=kb, ab=ab, vv=vv, decay=e_cur[cs_n - 1:cs_n, sl]))

    for q in pairs:
        gram = _mmb_nt(jnp.concatenate([q['kt'], q['rt']], axis=0),
                       jnp.concatenate([q['kb'], q['ab']], axis=0))
        q['akv'] = jnp.where(strict, gram[0:n2, 0:n2], 0.0)
        q['pw'] = jnp.where(strict, -gram[0:n2, n2:2 * n2], 0.0)
        q['brk'] = jnp.where(incl, gram[n2:2 * n2, 0:n2], 0.0)
        q['bra'] = jnp.where(incl, gram[n2:2 * n2, n2:2 * n2], 0.0)
        q['t'] = eye + q['pw']
    n = 1
    while 2 * n < cs_n:
        for q in pairs:
            q['pw'] = _mmb(q['pw'], q['pw'])
        for q in pairs:
            q['t'] = q['t'] + _mmb(q['t'], q['pw'])
        n *= 2
    for q in pairs:
        q['vt'] = q['vv'].T
        q['w1'] = _mmb(q['t'], q['kt'])
    for q in pairs:
        q['va'] = _mmb_nt(q['vt'], q['akv'])
    for i, q in enumerate(pairs):
        q['s'] = s_scr[i]
        q['ut'] = _mmb_nt(q['s'], q['w1']) + _mmb_nt(q['va'], q['t'])
    for i, q in enumerate(pairs):
        s_scr[i] = (q['s'] + _mmb(q['vt'], q['kb']) - _mmb(q['ut'], q['ab'])) * q['decay']
    for q in pairs:
        q['yt'] = _mmb_nt(q['s'], q['rt']) + _mmb_nt(q['vt'], q['brk']) - _mmb_nt(q['ut'], q['bra'])
    for b in range(bb):
        ys = []
        for p in range(HEAD_PAIRS):
            y2 = pairs[b * HEAD_PAIRS + p]['yt'].T
            ys.append(y2[0:cs_n, :] + y2[cs_n:n2, :])
        bon, g = tails[b]
        y_ref[b] = _tm_finish(jnp.concatenate(ys, axis=-1), bon, g, gnw_ref, gnb_ref)

    @pl.when(c == pl.num_programs(1) - 1)
    def _fin():
        wkv_ref[...] = s_scr[...].reshape(bb, HEAD_PAIRS, LANES, LANES)


def _timemix_chunk(cols, tm_params, *, bb):
    bsz, seq, _ = cols.shape
    tc = TM_CHUNK
    params = _tm_param_arrays(tm_params)
    y, wkv = pl.pallas_call(
        functools.partial(_timemix_chunk_kernel, bb=bb),
        grid=(bsz // bb, seq // tc),
        in_specs=[pl.BlockSpec((bb, tc, N_SHIFT), lambda i, c: (i, c, 0))] + [_full(p.shape) for p in params],
        out_specs=[pl.BlockSpec((bb, tc, A_WIDTH), lambda i, c: (i, c, 0)),
                   pl.BlockSpec((bb, HEAD_PAIRS, LANES, LANES), lambda i, c: (i, 0, 0, 0))],
        out_shape=[jax.ShapeDtypeStruct((bsz, seq, A_WIDTH), F32),
                   jax.ShapeDtypeStruct((bsz, HEAD_PAIRS, LANES, LANES), F32)],
        scratch_shapes=[pltpu.VMEM((bb * HEAD_PAIRS, LANES, LANES), F32), pltpu.VMEM((bb, 1, N_SHIFT), F32)],
        compiler_params=_cparams("arbitrary", "arbitrary"),
    )(cols, *params)
    wkv = wkv.reshape(bsz, HEAD_PAIRS, 2, A_HEAD_DIM, 2, A_HEAD_DIM)
    wkv = jnp.stack([wkv[:, :, 0, :, 0, :], wkv[:, :, 1, :, 1, :]], axis=2)
    return y, wkv.reshape(bsz, A_HEADS, A_HEAD_DIM, A_HEAD_DIM)


def _tm_param_arrays(tm_params):
    vec = lambda a: a.reshape(1, -1)
    mu, dbase, dup, ibase, iup, gup, kks, kim, bonus, gnw, gnb = tm_params
    zpad = jnp.zeros((LORA_DECAY, A_WIDTH), F32)
    dup_p = jnp.concatenate([dup, zpad], 0).astype(BF16)
    iup_p = jnp.concatenate([zpad, iup], 0).astype(BF16)
    return [vec(mu), vec(dbase), dup_p, vec(ibase), iup_p, gup.astype(BF16), vec(kks), vec(kim), vec(bonus),
            vec(gnw), vec(gnb)]


def _timemix_kernel(cols_ref, shift0_ref, wkv0_ref, mu_ref, dbase_ref, dup_ref, ibase_ref, iup_ref, gup_ref,
                    kks_ref, kim_ref, bonus_ref, gnw_ref, gnb_ref,
                    y_ref, wkv_ref,
                    s_scr, carry_scr, r_scr, w_scr, k_scr, kk_scr, ak_scr, g_scr, bon_scr, vpad_scr, vt_scr, yt_scr,
                    *, bb, tc, n_steps):
    c = pl.program_id(1)

    @pl.when(c == 0)
    def _init():
        s_scr[...] = wkv0_ref[...].reshape(bb * HEAD_PAIRS, A_HEAD_DIM, LANES)
        carry_scr[...] = shift0_ref[...]
        vpad_scr[...] = jnp.zeros_like(vpad_scr)

    prm = (mu_ref, dbase_ref, dup_ref, ibase_ref, iup_ref, gup_ref, kks_ref, kim_ref, bonus_ref)
    for b in range(bb):
        r, logw, k, v, kk, ak, g, bon = _tm_prep(cols_ref[b], carry_scr, b, n_steps, prm)
        for ref, val in ((r_scr, r), (w_scr, jnp.exp(logw)), (k_scr, k), (kk_scr, kk), (ak_scr, ak)):
            for p in range(HEAD_PAIRS):
                ref[b * HEAD_PAIRS + p] = val[:, p * LANES:(p + 1) * LANES]
        g_scr[b] = g
        bon_scr[b] = bon
        if tc == LANES:
            vt_scr[b] = v.T
        else:
            vpad_scr[b, 0:tc, :] = v
            vt_scr[b] = vpad_scr[b].T
    yt_scr[...] = jnp.zeros_like(yt_scr)

    lane = lax.broadcasted_iota(jnp.int32, (1, LANES), 1)
    lo = lane < A_HEAD_DIM
    hd = A_HEAD_DIM

    def seg_sum(x):
        s_lo = jnp.sum(jnp.where(lo, x, 0.0), axis=-1, keepdims=True)
        s_hi = jnp.sum(jnp.where(lo, 0.0, x), axis=-1, keepdims=True)
        return s_lo, s_hi

    n_sub = min(SUBLANES, n_steps)

    def block(i, carry):
        base = pl.multiple_of(i * SUBLANES, SUBLANES)
        for j in range(n_sub):
            tmask = lane == base + j
            for b in range(bb):
                for p in range(HEAD_PAIRS):
                    idx = b * HEAD_PAIRS + p
                    ra, rb = p * LANES, p * LANES + hd
                    r_t, w_t, k_t, kk_t, ak_t = (
                        jnp.broadcast_to(ref[idx, pl.ds(base, SUBLANES), :][j:j + 1, :], (hd, LANES))
                        for ref in (r_scr, w_scr, k_scr, kk_scr, ak_scr))
                    v_a = jnp.sum(jnp.where(tmask, vt_scr[b, ra:ra + hd, :], 0.0), axis=-1, keepdims=True)
                    v_b = jnp.sum(jnp.where(tmask, vt_scr[b, rb:rb + hd, :], 0.0), axis=-1, keepdims=True)
                    v_col = jnp.where(lo, v_a, v_b)
                    s = s_scr[idx]
                    u_a, u_b = seg_sum(s * kk_t)
                    s = s * w_t - jnp.where(lo, u_a, u_b) * ak_t + v_col * k_t
                    s_scr[idx] = s
                    y_a, y_b = seg_sum(s * r_t)
                    yt_scr[b, ra:ra + hd, :] = jnp.where(tmask, y_a, yt_scr[b, ra:ra + hd, :])
                    yt_scr[b, rb:rb + hd, :] = jnp.where(tmask, y_b, yt_scr[b, rb:rb + hd, :])
        return carry

    lax.fori_loop(0, n_steps // n_sub, block, 0)

    for b in range(bb):
        y_ref[b] = _tm_finish(yt_scr[b].T[0:tc, :], bon_scr[b], g_scr[b], gnw_ref, gnb_ref)

    @pl.when(c == pl.num_programs(1) - 1)
    def _fin():
        wkv_ref[...] = s_scr[...].reshape(bb, HEAD_PAIRS, A_HEAD_DIM, LANES)


def _pack_pairs(wkv):
    b = wkv.shape[0]
    x = wkv.reshape(b, HEAD_PAIRS, 2, A_HEAD_DIM, A_HEAD_DIM)
    return jnp.swapaxes(x, 2, 3).reshape(b, HEAD_PAIRS, A_HEAD_DIM, LANES)


def _unpack_pairs(x):
    b = x.shape[0]
    x = x.reshape(b, HEAD_PAIRS, A_HEAD_DIM, 2, A_HEAD_DIM)
    return jnp.swapaxes(x, 2, 3).reshape(b, A_HEADS, A_HEAD_DIM, A_HEAD_DIM)


def _timemix(cols, shift0, wkv0, tm_params, *, bb, tc, n_steps):
    bsz, seq, _ = cols.shape
    nb, nc = bsz // bb, seq // tc
    params = _tm_param_arrays(tm_params)
    scr = lambda *s: pltpu.VMEM(s, F32)
    y, wkv = pl.pallas_call(
        functools.partial(_timemix_kernel, bb=bb, tc=tc, n_steps=n_steps),
        grid=(nb, nc),
        in_specs=[pl.BlockSpec((bb, tc, N_SHIFT), lambda i, c: (i, c, 0)),
                  pl.BlockSpec((bb, 1, N_SHIFT), lambda i, c: (i, 0, 0)),
                  pl.BlockSpec((bb, HEAD_PAIRS, A_HEAD_DIM, LANES), lambda i, c: (i, 0, 0, 0))]
                 + [_full(p.shape) for p in params],
        out_specs=[pl.BlockSpec((bb, tc, A_WIDTH), lambda i, c: (i, c, 0)),
                   pl.BlockSpec((bb, HEAD_PAIRS, A_HEAD_DIM, LANES), lambda i, c: (i, 0, 0, 0))],
        out_shape=[jax.ShapeDtypeStruct((bsz, seq, A_WIDTH), F32),
                   jax.ShapeDtypeStruct((bsz, HEAD_PAIRS, A_HEAD_DIM, LANES), F32)],
        scratch_shapes=[scr(bb * HEAD_PAIRS, A_HEAD_DIM, LANES), scr(bb, 1, N_SHIFT)]
                       + [scr(bb * HEAD_PAIRS, tc, LANES)] * 5 + [scr(bb, tc, A_WIDTH)] * 2
                       + [scr(bb, LANES, A_WIDTH), scr(bb, A_WIDTH, LANES), scr(bb, A_WIDTH, LANES)],
        compiler_params=_cparams("arbitrary", "arbitrary"),
    )(cols, shift0.reshape(bsz, 1, N_SHIFT), _pack_pairs(wkv0), *params)
    return y, _unpack_pairs(wkv)


def _bucket(n):
    max_exact = NUM_BUCKETS // 2
    nf = jnp.maximum(n, 1).astype(F32)
    large = max_exact + jnp.floor(jnp.log(nf / max_exact) / math.log(MAX_DISTANCE / max_exact)
                                  * (NUM_BUCKETS - max_exact)).astype(jnp.int32)
    return jnp.where(n < max_exact, n, jnp.minimum(large, NUM_BUCKETS - 1))


def _prompt_bias_kernel(table_ref, out_ref):
    h = pl.program_id(0)
    t = ATT_T
    rr = lax.broadcasted_iota(jnp.int32, (t, t), 0)
    cc = lax.broadcasted_iota(jnp.int32, (t, t), 1)
    far = table_ref[NUM_BUCKETS - 1, h]
    for i in range(2):
        d = (1 - i) * t + rr - cc
        bkt = _bucket(jnp.maximum(d, 0))
        bias = jnp.zeros((t, t), F32)
        for bk in range(NUM_BUCKETS):
            bias = jnp.where(bkt == bk, table_ref[bk, h], bias)
        out_ref[0, i] = jnp.where(d >= 0, (bias - far) * LOG2E, NEG_BIG)


def _prompt_bias(table):
    t = ATT_T
    return pl.pallas_call(
        _prompt_bias_kernel,
        grid=(B_HEADS,),
        in_specs=[pl.BlockSpec(memory_space=pltpu.SMEM)],
        out_specs=pl.BlockSpec((1, 2, t, t), lambda h: (h, 0, 0, 0)),
        out_shape=jax.ShapeDtypeStruct((B_HEADS, 2, t, t), F32),
        compiler_params=_cparams("arbitrary"),
    )(table)


def _sample_rows(n_new):
    return n_new * 2 * B_HEADS


def _sample_bias_kernel(table_ref, out_ref, *, n_new, page):
    nr = _sample_rows(n_new)
    rr = lax.broadcasted_iota(jnp.int32, (nr, LANES), 0)
    cc = lax.broadcasted_iota(jnp.int32, (nr, LANES), 1)
    tok = rr % n_new
    head = rr // (2 * n_new)

    def lookup(d):
        bkt = _bucket(jnp.maximum(d, 0))
        bias = jnp.zeros((nr, LANES), F32)
        for hh in range(B_HEADS):
            for bk in range(NUM_BUCKETS):
                bias = jnp.where((bkt == bk) & (head == hh), table_ref[bk, hh], bias)
        return bias

    far = lookup(jnp.full((nr, LANES), MAX_DISTANCE, jnp.int32))
    out_ref[0] = far * LOG2E
    out_ref[1] = lookup(tok + page - cc) * LOG2E
    d_self = tok - cc
    out_ref[2] = jnp.where((d_self >= 0) & (cc < n_new), lookup(d_self) * LOG2E, NEG_BIG)


def _sample_bias(table, n_new, page):
    nr = _sample_rows(n_new)
    return pl.pallas_call(
        functools.partial(_sample_bias_kernel, n_new=n_new, page=page),
        in_specs=[pl.BlockSpec(memory_space=pltpu.SMEM)],
        out_specs=pl.BlockSpec(memory_space=pltpu.VMEM),
        out_shape=jax.ShapeDtypeStruct((3, nr, LANES), F32),
    )(table)


def _lambda(lq1_ref, lk1_ref, lq2_ref, lk2_ref, lam_init):
    s1 = jnp.sum(lq1_ref[...] * lk1_ref[...], axis=-1, keepdims=True)
    s2 = jnp.sum(lq2_ref[...] * lk2_ref[...], axis=-1, keepdims=True)
    return jnp.exp(s1) - jnp.exp(s2) + lam_init


def _nt_dot(a, b):
    return lax.dot_general(a, b, (((1,), (1,)), ((), ())), preferred_element_type=F32)


def _attn_prompt_kernel(q_ref, k_ref, v_ref, bias_ref, lq1_ref, lk1_ref, lq2_ref, lk2_ref, subln_ref, o_ref,
                        acc_scr, *, lam_init):
    i = pl.program_id(2)
    t = ATT_T
    nh = ATT_HEADS
    lane = lax.broadcasted_iota(jnp.int32, (1, LANES), 1)
    first = lane < B_HEAD_DIM
    chains = [(hh, c) for hh in range(nh) for c in range(2)]
    qs = []
    for hh, c in chains:
        q = q_ref[0, :, hh * B_V_DIM:(hh + 1) * B_V_DIM]
        qs.append((jnp.where(first, q, 0.0) if c == 0 else jnp.where(first, 0.0, q)).astype(BF16))
    acc_scr[...] = jnp.zeros_like(acc_scr)

    def tile(j, carry, near):
        ms, ls = carry
        off = pl.multiple_of(j * t, t)
        kts = [k_ref[0, pl.ds(off, t), hh * B_V_DIM:(hh + 1) * B_V_DIM] for hh in range(nh)]
        vts = [v_ref[0, pl.ds(off, t), hh * B_V_DIM:(hh + 1) * B_V_DIM] for hh in range(nh)]
        ss = [_nt_dot(qs[n], kts[hh]) for n, (hh, _) in enumerate(chains)]
        if near:
            biases = [bias_ref[hh, j - i + 1] for hh in range(nh)]
            ss = [s + biases[hh] for s, (hh, _) in zip(ss, chains)]
        m_new = [jnp.maximum(m, jnp.max(s, axis=-1, keepdims=True)) for m, s in zip(ms, ss)]
        ps = [jnp.exp2(s - m) for s, m in zip(ss, m_new)]
        alphas = [jnp.exp2(m - mn) for m, mn in zip(ms, m_new)]
        ls = tuple(a * l + jnp.sum(p, axis=-1, keepdims=True) for a, l, p in zip(alphas, ls, ps))
        for n, (hh, _) in enumerate(chains):
            acc_scr[n] = alphas[n] * acc_scr[n] + jnp.dot(ps[n].astype(BF16), vts[hh],
                                                          preferred_element_type=F32)
        return tuple(m_new), ls

    m0 = tuple(jnp.full((t, 1), NEG_BIG, F32) for _ in chains)
    l0 = tuple(jnp.zeros((t, 1), F32) for _ in chains)
    n_far = jnp.maximum(i - 1, 0)
    carry = lax.fori_loop(0, n_far, functools.partial(tile, near=False), (m0, l0))
    _, ls = lax.fori_loop(n_far, i + 1, functools.partial(tile, near=True), carry)
    lam = _lambda(lq1_ref, lk1_ref, lq2_ref, lk2_ref, lam_init)
    for hh in range(nh):
        o = acc_scr[2 * hh] * (1.0 / ls[2 * hh]) - acc_scr[2 * hh + 1] * (lam / ls[2 * hh + 1])
        ms = jnp.sum(o * o, axis=-1, keepdims=True) * (1.0 / B_V_DIM)
        o_ref[0, :, hh * B_V_DIM:(hh + 1) * B_V_DIM] = (o * lax.rsqrt(ms + RMS_EPS) * subln_ref[...]
                                                        * (1.0 - lam_init))


def _attn_prompt(q, k16, v16, bias, lam_params, subln, lam_init):
    bsz, seq, _ = q.shape
    t = ATT_T
    nh = ATT_HEADS
    w = nh * B_V_DIM
    lam_specs = [_full((1, B_HEAD_DIM))] * 4
    return pl.pallas_call(
        functools.partial(_attn_prompt_kernel, lam_init=lam_init),
        grid=(bsz, B_HEADS // nh, seq // t),
        in_specs=[pl.BlockSpec((1, t, w), lambda b, h, i: (b, i, h)),
                  pl.BlockSpec((1, seq, w), lambda b, h, i: (b, 0, h)),
                  pl.BlockSpec((1, seq, w), lambda b, h, i: (b, 0, h)),
                  pl.BlockSpec((nh, 2, t, t), lambda b, h, i: (h, 0, 0, 0))]
                 + lam_specs + [_full((1, B_V_DIM))],
        out_specs=pl.BlockSpec((1, t, w), lambda b, h, i: (b, i, h)),
        out_shape=jax.ShapeDtypeStruct((bsz, seq, B_WIDTH), F32),
        scratch_shapes=[pltpu.VMEM((2 * nh, t, B_V_DIM), F32)],
        compiler_params=_cparams("arbitrary", "arbitrary", "arbitrary"),
    )(q, k16, v16, bias, *lam_params, subln)


def _attn_sample_kernel(pt_ref, q_ref, kn_ref, vn_ref, sb_ref, lq1_ref, lk1_ref, lq2_ref, lk2_ref, subln_ref,
                        *rest, n_new, lam_init):
    pg = PAGES_PER_STEP
    k_refs, v_refs = rest[:pg], rest[pg:2 * pg]
    o_ref = rest[2 * pg]
    qb_scr, m_scr, l_scr, acc_scr, kn_scr, vn_scr = rest[2 * pg + 1:]
    del pt_ref
    g = pl.program_id(1)
    ng = pl.num_programs(1)
    nr = _sample_rows(n_new)
    rph = 2 * n_new
    page = kn_scr.shape[0]

    @pl.when(g == 0)
    def _init():
        rr = lax.broadcasted_iota(jnp.int32, (rph, B_WIDTH), 0)
        cc = lax.broadcasted_iota(jnp.int32, (rph, B_WIDTH), 1)
        q = jnp.where(rr < n_new, q_ref[0], 0.0)
        q2 = q + pltpu.roll(q, n_new, axis=0)
        for h in range(B_HEADS):
            sel = (cc // B_HEAD_DIM) == 2 * h + rr // n_new
            qb_scr[h * rph:(h + 1) * rph, :] = jnp.where(sel, q2, 0.0)
        m_scr[...] = jnp.full_like(m_scr, NEG_BIG)
        l_scr[...] = jnp.zeros_like(l_scr)
        acc_scr[...] = jnp.zeros_like(acc_scr)
        kn_scr[...] = jnp.zeros_like(kn_scr)
        vn_scr[...] = jnp.zeros_like(vn_scr)
        kn_scr[0:SAMPLE_T_PAD, :] = kn_ref[0]
        vn_scr[0:SAMPLE_T_PAD, :] = vn_ref[0]

    def update(s_list, v_list):
        m = m_scr[...]
        m_new = m
        for s in s_list:
            m_new = jnp.maximum(m_new, jnp.max(s, axis=-1, keepdims=True))
        alpha = jnp.exp2(m - m_new)
        l = alpha * l_scr[...]
        acc = [alpha[h * rph:(h + 1) * rph] * acc_scr[h * rph:(h + 1) * rph, :] for h in range(B_HEADS)]
        for s, v_heads in zip(s_list, v_list):
            p = jnp.exp2(s - m_new)
            l = l + jnp.sum(p, axis=-1, keepdims=True)
            for h in range(B_HEADS):
                acc[h] = acc[h] + jnp.dot(p[h * rph:(h + 1) * rph, :].astype(BF16), v_heads[h],
                                          preferred_element_type=F32)
        m_scr[...] = m_new
        l_scr[...] = l
        for h in range(B_HEADS):
            acc_scr[h * rph:(h + 1) * rph, :] = acc[h]

    qb = qb_scr[...].astype(BF16)
    far = sb_ref[0]
    s_list, v_list = [], []
    for i in range(pg):
        bias = far
        if i == pg - 1:
            bias = sb_ref[jnp.where(g == ng - 1, 1, 0)]
        s_list.append(jnp.dot(qb, k_refs[i][0].astype(BF16), preferred_element_type=F32) + bias)
        v_list.append([v_refs[i][0, pl.ds(h, page, stride=B_HEADS), :].astype(BF16) for h in range(B_HEADS)])
    update(s_list, v_list)

    @pl.when(g == ng - 1)
    def _fin():
        vn = vn_scr[...].astype(BF16)
        update([_nt_dot(qb, kn_scr[...].astype(BF16)) + sb_ref[2]],
               [[vn[:, h * B_V_DIM:(h + 1) * B_V_DIM] for h in range(B_HEADS)]])
        lam = _lambda(lq1_ref, lk1_ref, lq2_ref, lk2_ref, lam_init)
        rr = lax.broadcasted_iota(jnp.int32, (nr, 1), 0)
        coef = jnp.where(rr % rph < n_new, 1.0, -lam) / l_scr[...]
        w = acc_scr[...] * coef
        parts = []
        for h in range(B_HEADS):
            wh = w[h * rph:(h + 1) * rph, :]
            oh = wh + pltpu.roll(wh, n_new, axis=0)
            ms = jnp.sum(oh * oh, axis=-1, keepdims=True) * (1.0 / B_V_DIM)
            parts.append(oh * lax.rsqrt(ms + RMS_EPS) * subln_ref[...] * (1.0 - lam_init))
        o_ref[0] = jnp.concatenate(parts, axis=-1)


def _attn_sample(q, k_new, v_new, cache_k, cache_v, page_table, sbias, lam_params, subln, lam_init, n_new):
    bsz = q.shape[0]
    n_pages = page_table.shape[1]
    page = cache_k.shape[2]
    pg = PAGES_PER_STEP
    nr = _sample_rows(n_new)
    assert 2 * n_new == SAMPLE_T_PAD == SUBLANES
    tok_spec = pl.BlockSpec((1, SAMPLE_T_PAD, B_WIDTH), lambda b, g, pt: (b, 0, 0))

    def page_spec(i):
        return pl.BlockSpec((1, B_WIDTH, page), lambda b, g, pt: (pt[b, g * pg + i], 0, 0))

    const = lambda shape: pl.BlockSpec(shape, lambda b, g, pt: (0,) * len(shape))
    grid_spec = pltpu.PrefetchScalarGridSpec(
        num_scalar_prefetch=1,
        grid=(bsz, n_pages // pg),
        in_specs=[tok_spec, tok_spec, tok_spec, const((3, nr, LANES))]
                 + [const((1, B_HEAD_DIM))] * 4 + [const((1, B_V_DIM))]
                 + [page_spec(i) for i in range(pg)] * 2,
        out_specs=tok_spec,
        scratch_shapes=[pltpu.VMEM((nr, B_WIDTH), F32), pltpu.VMEM((nr, 1), F32), pltpu.VMEM((nr, 1), F32),
                        pltpu.VMEM((nr, B_V_DIM), F32), pltpu.VMEM((page, B_WIDTH), F32),
                        pltpu.VMEM((page, B_WIDTH), F32)],
    )
    return pl.pallas_call(
        functools.partial(_attn_sample_kernel, n_new=n_new, lam_init=lam_init),
        grid_spec=grid_spec,
        out_shape=jax.ShapeDtypeStruct((bsz, SAMPLE_T_PAD, B_WIDTH), F32),
        compiler_params=_cparams("arbitrary", "arbitrary"),
    )(page_table, q, k_new, v_new, sbias, *lam_params, subln, *([cache_k] * pg), *([cache_v] * pg))


def _layer_norm(z, w, b):
    inv_d = 1.0 / D_MODEL
    mu = jnp.sum(z, axis=-1, keepdims=True) * inv_d
    d = z - mu
    var = jnp.sum(d * d, axis=-1, keepdims=True) * inv_d
    return d * lax.rsqrt(var + LN_EPS) * w + b


def _merge_kernel(x_ref, ya_ref, ob_ref, gates_ref, wa_ref, wb_ref, wo_ref, lnw_ref, lnb_ref, o_ref):
    ga = _sigmoid(gates_ref[:, 0:D_MODEL])
    gb = _sigmoid(gates_ref[:, D_MODEL:2 * D_MODEL])
    merged = (ga * jnp.dot(ya_ref[...].astype(BF16), wa_ref[...], preferred_element_type=F32)
              + gb * jnp.dot(ob_ref[...].astype(BF16), wb_ref[...], preferred_element_type=F32))
    mix = jnp.dot(merged.astype(BF16), wo_ref[...], preferred_element_type=F32)
    o_ref[...] = _layer_norm(DEEPNORM_ALPHA * x_ref[...] + mix, lnw_ref[...], lnb_ref[...])


def _merge(x2d, ya, ob, gates, wa, wb, wo, lnw, lnb):
    m = x2d.shape[0]
    tm = min(MERGE_TM, m)
    row = lambda w: pl.BlockSpec((tm, w), lambda i: (i, 0))
    return pl.pallas_call(
        _merge_kernel,
        grid=(m // tm,),
        in_specs=[row(D_MODEL), row(A_WIDTH), row(B_WIDTH), row(2 * D_MODEL),
                  _full(wa.shape), _full(wb.shape), _full(wo.shape), _full((1, D_MODEL)), _full((1, D_MODEL))],
        out_specs=row(D_MODEL),
        out_shape=jax.ShapeDtypeStruct((m, D_MODEL), F32),
        compiler_params=_cparams("arbitrary"),
    )(x2d, ya, ob, gates, wa, wb, wo, lnw.reshape(1, -1), lnb.reshape(1, -1))


def _gelu(x):
    return 0.5 * x * (1.0 + lax.erf(x * (2.0 ** -0.5)))


def _ffn_prompt_kernel(x_ref, wu_ref, wg_ref, cwu_ref, cwg_ref, cbu_ref, cbg_ref, wd_ref, lnw_ref, lnb_ref,
                       o_ref, tu_ref, tg_ref, acc_scr, cu_scr, cg_scr):
    i, j = pl.program_id(1), pl.program_id(2)
    tm = x_ref.shape[1]
    rows = lax.broadcasted_iota(jnp.int32, (tm, 1), 0)
    xb = x_ref[0].astype(BF16)

    @pl.when(i == 0)
    def _reset():
        cu_scr[j] = jnp.zeros((SUBLANES, FFN_TF), F32)
        cg_scr[j] = jnp.zeros((SUBLANES, FFN_TF), F32)

    def conv(w_ref, cw_ref, cb_ref, carry_scr, tail_ref):
        h = jnp.dot(xb, w_ref[...], preferred_element_type=F32)
        carry = carry_scr[j]
        c6, c7 = carry[SUBLANES - 2:SUBLANES - 1, :], carry[SUBLANES - 1:SUBLANES, :]
        h1 = jnp.where(rows == 0, c7, pltpu.roll(h, 1, axis=0))
        h2 = jnp.where(rows == 0, c6, jnp.where(rows == 1, c7, pltpu.roll(h, 2, axis=0)))
        tail = h[tm - SUBLANES:tm, :]
        carry_scr[j] = tail
        tail_ref[0, 0] = tail
        return cb_ref[...] + (cw_ref[0:1, :] * h2 + cw_ref[1:2, :] * h1 + cw_ref[2:3, :] * h)

    u = conv(wu_ref, cwu_ref, cbu_ref, cu_scr, tu_ref)
    g = conv(wg_ref, cwg_ref, cbg_ref, cg_scr, tg_ref)
    part = jnp.dot((_gelu(g) * u).astype(BF16), wd_ref[...], preferred_element_type=F32)

    @pl.when(j == 0)
    def _first():
        acc_scr[...] = part

    @pl.when(j > 0)
    def _rest():
        acc_scr[...] += part

    @pl.when(j == pl.num_programs(2) - 1)
    def _fin():
        o_ref[0] = _layer_norm(DEEPNORM_ALPHA * x_ref[0] + acc_scr[...], lnw_ref[...], lnb_ref[...])


def _ffn_prompt(x, w_up, conv_w, conv_b, w_down, lnw, lnb):
    bsz, seq, _ = x.shape
    tm, tf = FFN_TM, FFN_TF
    nf = FFN_DIM // tf
    nt = seq // tm
    conv_b = conv_b.reshape(1, -1)
    ucol = lambda r: pl.BlockSpec((r, tf), lambda b, i, j: (0, j))
    gcol = lambda r: pl.BlockSpec((r, tf), lambda b, i, j: (0, nf + j))
    tail = pl.BlockSpec((1, 1, SUBLANES, tf), lambda b, i, j: (b, i, 0, j))
    return pl.pallas_call(
        _ffn_prompt_kernel,
        grid=(bsz, nt, nf),
        in_specs=[pl.BlockSpec((1, tm, D_MODEL), lambda b, i, j: (b, i, 0)),
                  ucol(D_MODEL), gcol(D_MODEL), ucol(CONV_WIDTH), gcol(CONV_WIDTH), ucol(1), gcol(1),
                  pl.BlockSpec((tf, D_MODEL), lambda b, i, j: (j, 0)),
                  pl.BlockSpec((1, D_MODEL), lambda b, i, j: (0, 0)),
                  pl.BlockSpec((1, D_MODEL), lambda b, i, j: (0, 0))],
        out_specs=[pl.BlockSpec((1, tm, D_MODEL), lambda b, i, j: (b, i, 0)), tail, tail],
        out_shape=[jax.ShapeDtypeStruct((bsz, seq, D_MODEL), F32),
                   jax.ShapeDtypeStruct((bsz, nt, SUBLANES, FFN_DIM), F32),
                   jax.ShapeDtypeStruct((bsz, nt, SUBLANES, FFN_DIM), F32)],
        scratch_shapes=[pltpu.VMEM((tm, D_MODEL), F32), pltpu.VMEM((nf, SUBLANES, tf), F32),
                        pltpu.VMEM((nf, SUBLANES, tf), F32)],
        compiler_params=_cparams("arbitrary", "arbitrary", "arbitrary"),
    )(x, w_up, w_up, conv_w, conv_w, conv_b, conv_b, w_down, lnw.reshape(1, -1), lnb.reshape(1, -1))


def _ffn_sample_kernel(x_ref, bu_ref, bg_ref, wu_ref, wg_ref, cwu_ref, cwg_ref, cbu_ref, cbg_ref, wd_ref,
                       lnw_ref, lnb_ref, o_ref, tu_ref, tg_ref, acc_scr, *, n_new, bsz):
    j = pl.program_id(0)
    m = n_new * bsz
    nbuf = (CONV_WIDTH - 1) * bsz
    xb = x_ref[...].astype(BF16)

    def conv(w_ref, cw_ref, cb_ref, buf_ref, tail_ref):
        h = jnp.dot(xb, w_ref[...], preferred_element_type=F32)
        hp = jnp.concatenate([buf_ref[...], h], axis=0)
        tail_ref[...] = hp[m:m + nbuf, :]
        return cb_ref[...] + (cw_ref[0:1, :] * hp[0:m, :] + cw_ref[1:2, :] * hp[bsz:bsz + m, :]
                              + cw_ref[2:3, :] * hp[2 * bsz:2 * bsz + m, :])

    u = conv(wu_ref, cwu_ref, cbu_ref, bu_ref, tu_ref)
    g = conv(wg_ref, cwg_ref, cbg_ref, bg_ref, tg_ref)
    part = jnp.dot((_gelu(g) * u).astype(BF16), wd_ref[...], preferred_element_type=F32)

    @pl.when(j == 0)
    def _first():
        acc_scr[...] = part

    @pl.when(j > 0)
    def _rest():
        acc_scr[...] += part

    @pl.when(j == pl.num_programs(0) - 1)
    def _fin():
        o_ref[...] = _layer_norm(DEEPNORM_ALPHA * x_ref[...] + acc_scr[...], lnw_ref[...], lnb_ref[...])


def _ffn_sample(x_pm, buf_pm, w_up, conv_w, conv_b, w_down, lnw, lnb, n_new, bsz):
    m = n_new * bsz
    nbuf = (CONV_WIDTH - 1) * bsz
    tf = FFN_TF
    nf = FFN_DIM // tf
    conv_b = conv_b.reshape(1, -1)
    ucol = lambda r: pl.BlockSpec((r, tf), lambda j: (0, j))
    gcol = lambda r: pl.BlockSpec((r, tf), lambda j: (0, nf + j))
    return pl.pallas_call(
        functools.partial(_ffn_sample_kernel, n_new=n_new, bsz=bsz),
        grid=(nf,),
        in_specs=[_full((m, D_MODEL)), ucol(nbuf), gcol(nbuf), ucol(D_MODEL), gcol(D_MODEL),
                  ucol(CONV_WIDTH), gcol(CONV_WIDTH), ucol(1), gcol(1),
                  pl.BlockSpec((tf, D_MODEL), lambda j: (j, 0)), _full((1, D_MODEL)), _full((1, D_MODEL))],
        out_specs=[_full((m, D_MODEL)), ucol(nbuf), ucol(nbuf)],
        out_shape=[jax.ShapeDtypeStruct((m, D_MODEL), F32), jax.ShapeDtypeStruct((nbuf, FFN_DIM), F32),
                   jax.ShapeDtypeStruct((nbuf, FFN_DIM), F32)],
        scratch_shapes=[pltpu.VMEM((m, D_MODEL), F32)],
        compiler_params=_cparams("arbitrary"),
    )(x_pm, buf_pm, buf_pm, w_up, w_up, conv_w, conv_w, conv_b, conv_b, w_down,
      lnw.reshape(1, -1), lnb.reshape(1, -1))


def kernel(x_prompt, x_sample, cache_k, cache_v, state_wkv, state_shift, state_conv, page_table, rel_bias_table, w_in, mu_shift, decay_base, decay_up, iclr_base, iclr_up, gate_up, kk_scale, k_iclr_mix, bonus_rk, gn_w, gn_b, lam_q1, lam_k1, lam_q2, lam_k2, subln_w, w_branch_a, w_branch_b, w_out, ln1_w, ln1_b, w_up, conv_w, conv_b, w_down, ln2_w, ln2_b):
    assert w_in.shape[0] == DEPTH == 1
    l = 0
    lam_init = 0.8 - 0.6 * math.exp(-0.3 * l)
    bp, seq, _ = x_prompt.shape
    bs, n_new, _ = x_sample.shape
    page = cache_k.shape[2]
    assert page == LANES and page >= MAX_DISTANCE and ATT_T >= MAX_DISTANCE and n_new <= SAMPLE_T_PAD

    w_in_b = w_in[l].astype(BF16)
    wa, wb, wo = w_branch_a[l].astype(BF16), w_branch_b[l].astype(BF16), w_out[l].astype(BF16)
    w_up_b, w_down_b = w_up[l].astype(BF16), w_down[l].astype(BF16)
    tm_params = (mu_shift[l], decay_base[l], decay_up[l], iclr_base[l], iclr_up[l], gate_up[l], kk_scale[l],
                 k_iclr_mix[l], bonus_rk[l].reshape(-1), gn_w[l], gn_b[l])
    lam_params = [p[l].reshape(1, -1) for p in (lam_q1, lam_k1, lam_q2, lam_k2)]
    subln = subln_w[l].reshape(1, -1)

    mp = bp * seq
    xp2 = x_prompt.reshape(mp, D_MODEL)
    cols, q, k, v, gates, k16, v16 = _proj(xp2, w_in_b, True)
    cols3 = cols.reshape(bp, seq, N_SHIFT)
    ya, wkv_p = _timemix_chunk(cols3, tm_params, bb=bp)
    r3 = lambda a: a.reshape(bp, seq, B_WIDTH)
    ob = _attn_prompt(r3(q), r3(k16), r3(v16), _prompt_bias(rel_bias_table), lam_params, subln, lam_init)
    x1 = _merge(xp2, ya.reshape(mp, A_WIDTH), ob.reshape(mp, B_WIDTH), gates, wa, wb, wo, ln1_w[l], ln1_b[l])
    y_p, tu, tg = _ffn_prompt(x1.reshape(bp, seq, D_MODEL), w_up_b, conv_w[l], conv_b[l], w_down_b,
                              ln2_w[l], ln2_b[l])
    nbuf = CONV_WIDTH - 1
    k_prompt = k.reshape(1, bp, seq, B_HEADS, 2, B_HEAD_DIM)
    v_prompt = v.reshape(1, bp, seq, B_HEADS, B_V_DIM)
    shift_prompt = cols3[:, seq - 1][None]
    conv_prompt = jnp.concatenate([tu[:, -1, SUBLANES - nbuf:], tg[:, -1, SUBLANES - nbuf:]], axis=-1)[None]

    tp = SAMPLE_T_PAD
    xs8 = jnp.pad(x_sample, ((0, 0), (0, tp - n_new), (0, 0))).reshape(bs * tp, D_MODEL)
    cols, q, k, v, gates = _proj(xs8, w_in_b, False)
    cols3 = cols.reshape(bs, tp, N_SHIFT)
    ya, wkv_s = _timemix(cols3, state_shift[l], state_wkv[l], tm_params, bb=4, tc=tp, n_steps=n_new)
    r3 = lambda a: a.reshape(bs, tp, B_WIDTH)
    n_phys = cache_k.shape[1]
    cache_kt = jnp.transpose(cache_k[l], (0, 2, 3, 4, 1)).reshape(n_phys, B_WIDTH, page)
    cache_v2 = cache_v[l].reshape(n_phys, page * B_HEADS, B_V_DIM)
    ob = _attn_sample(r3(q), r3(k), r3(v), cache_kt, cache_v2, page_table,
                      _sample_bias(rel_bias_table, n_new, page), lam_params, subln, lam_init, n_new)
    x1 = _merge(xs8, ya.reshape(bs * tp, A_WIDTH), ob.reshape(bs * tp, B_WIDTH), gates, wa, wb, wo,
                ln1_w[l], ln1_b[l])
    x1_pm = jnp.swapaxes(x1.reshape(bs, tp, D_MODEL)[:, :n_new], 0, 1).reshape(n_new * bs, D_MODEL)
    buf_pm = jnp.swapaxes(state_conv[l], 0, 1).reshape(nbuf * bs, 2 * FFN_DIM)
    y_pm, tu, tg = _ffn_sample(x1_pm, buf_pm, w_up_b, conv_w[l], conv_b[l], w_down_b, ln2_w[l], ln2_b[l],
                               n_new, bs)
    y_s = jnp.swapaxes(y_pm.reshape(n_new, bs, D_MODEL), 0, 1)
    conv_sample = jnp.swapaxes(jnp.concatenate([tu, tg], axis=-1).reshape(nbuf, bs, 2 * FFN_DIM), 0, 1)[None]
    k_sample = k.reshape(bs, tp, B_HEADS, 2, B_HEAD_DIM)[:, :n_new][None]
    v_sample = v.reshape(bs, tp, B_HEADS, B_V_DIM)[:, :n_new][None]
    shift_sample = cols3[:, n_new - 1][None]

    return (y_p, y_s, k_prompt, v_prompt, wkv_p[None], shift_prompt, conv_prompt,
            k_sample, v_sample, wkv_s[None], shift_sample, conv_sample)
```

```python
import functools
import math

import jax
import jax.numpy as jnp
from jax import lax
from jax.experimental import pallas as pl
from jax.experimental.pallas import tpu as pltpu

F32 = jnp.float32
BF16 = jnp.bfloat16

D_MODEL = 1024
DEPTH = 1
A_WIDTH = D_MODEL // 2
A_HEAD_DIM = 64
A_HEADS = A_WIDTH // A_HEAD_DIM
LORA_DECAY = 64
LORA_ICLR = 64
LORA_GATE = 128
N_SHIFT = 3 * A_WIDTH + LORA_DECAY + LORA_ICLR + LORA_GATE
B_WIDTH = D_MODEL // 2
B_HEAD_DIM = 64
B_V_DIM = 2 * B_HEAD_DIM
B_HEADS = B_WIDTH // B_V_DIM
N_COLS = N_SHIFT + 3 * B_WIDTH + 2 * D_MODEL
NUM_BUCKETS = 32
MAX_DISTANCE = 128
FFN_DIM = ((8 * D_MODEL // 3 + 255) // 256) * 256
CONV_WIDTH = 3
LN_EPS = 1e-5
GN_EPS = 64e-5
RMS_EPS = 1e-5
DEEPNORM_ALPHA = (2 * DEPTH) ** 0.25

LANES = 128
SUBLANES = 8
VMEM_LIMIT = 56 * 1024 * 1024

NEG_BIG = -1e30
LOG2E = math.log2(math.e)
HEAD_PAIRS = A_HEADS // 2

PROJ_TM = 256
TM_CHUNK = 64
ATT_T = 512
ATT_HEADS = 4
PAGES_PER_STEP = 32
SAMPLE_T_PAD = SUBLANES
MERGE_TM = 512
FFN_TM = 512
FFN_TF = FFN_DIM // 2


def _cparams(*sem):
    return pltpu.CompilerParams(dimension_semantics=sem, vmem_limit_bytes=VMEM_LIMIT)


def _full(shape):
    n = len(shape)
    return pl.BlockSpec(shape, lambda *_: (0,) * n)


def _proj_kernel(x_ref, w_ref, cols_ref, q_ref, k_ref, v_ref, gates_ref, *mxu_refs, cache_shaped):
    xb = x_ref[...].astype(BF16)

    def mm(lo, hi):
        return jnp.dot(xb, w_ref[:, lo:hi], preferred_element_type=F32)

    o = N_SHIFT
    cols_ref[...] = mm(0, o)
    q_ref[...] = mm(o, o + B_WIDTH) * (B_HEAD_DIM ** -0.5 * LOG2E)
    k = mm(o + B_WIDTH, o + 2 * B_WIDTH)
    v = mm(o + 2 * B_WIDTH, o + 3 * B_WIDTH)
    if cache_shaped:
        for h in range(B_HEADS):
            v_ref[:, h, :] = v[:, h * B_V_DIM:(h + 1) * B_V_DIM]
            for c in range(2):
                lo = h * B_V_DIM + c * B_HEAD_DIM
                k_ref[:, h, c, :] = k[:, lo:lo + B_HEAD_DIM]
        k16_ref, v16_ref = mxu_refs
        k16_ref[...] = k.astype(BF16)
        v16_ref[...] = v.astype(BF16)
    else:
        k_ref[...] = k
        v_ref[...] = v
    gates_ref[...] = mm(o + 3 * B_WIDTH, N_COLS)


def _proj(x2d, w_in_bf16, cache_shaped):
    m = x2d.shape[0]
    tm = min(PROJ_TM, m)
    row = lambda w, dt: (pl.BlockSpec((tm, w), lambda i: (i, 0)), jax.ShapeDtypeStruct((m, w), dt))
    outs = [row(N_SHIFT, F32), row(B_WIDTH, F32)]
    if cache_shaped:
        outs += [(pl.BlockSpec((tm, B_HEADS, 2, B_HEAD_DIM), lambda i: (i, 0, 0, 0)),
                  jax.ShapeDtypeStruct((m, B_HEADS, 2, B_HEAD_DIM), F32)),
                 (pl.BlockSpec((tm, B_HEADS, B_V_DIM), lambda i: (i, 0, 0)),
                  jax.ShapeDtypeStruct((m, B_HEADS, B_V_DIM), F32)),
                 row(2 * D_MODEL, F32), row(B_WIDTH, BF16), row(B_WIDTH, BF16)]
    else:
        outs += [row(B_WIDTH, F32), row(B_WIDTH, F32), row(2 * D_MODEL, F32)]
    return pl.pallas_call(
        functools.partial(_proj_kernel, cache_shaped=cache_shaped),
        grid=(m // tm,),
        in_specs=[pl.BlockSpec((tm, D_MODEL), lambda i: (i, 0)), _full((D_MODEL, N_COLS))],
        out_specs=[s for s, _ in outs],
        out_shape=[t for _, t in outs],
        compiler_params=_cparams("arbitrary"),
    )(x2d, w_in_bf16)


def _pair_lo_mask():
    return lax.broadcasted_iota(jnp.int32, (1, LANES), 1) < A_HEAD_DIM


def _head_sum(x):
    lo = _pair_lo_mask()
    out = []
    for p in range(HEAD_PAIRS):
        xp = x[:, p * LANES:(p + 1) * LANES]
        s_lo = jnp.sum(jnp.where(lo, xp, 0.0), axis=-1, keepdims=True)
        s_hi = jnp.sum(jnp.where(lo, 0.0, xp), axis=-1, keepdims=True)
        out.append(jnp.where(lo, s_lo, s_hi))
    return jnp.concatenate(out, axis=-1)


def _softplus(z):
    return jnp.maximum(z, 0.0) + jnp.log1p(jnp.exp(-jnp.abs(z)))


def _sigmoid(z):
    return 1.0 / (1.0 + jnp.exp(-z))


def _tm_prep(cols, carry_scr, b, n_valid, prm):
    mu_ref, dbase_ref, dup_ref, ibase_ref, iup_ref, gup_ref, kks_ref, kim_ref, bonus_ref = prm
    aw = A_WIDTH
    rows = lax.broadcasted_iota(jnp.int32, (cols.shape[0], 1), 0)
    prev = jnp.where(rows == 0, carry_scr[b], pltpu.roll(cols, 1, axis=0))
    carry_scr[b] = cols[n_valid - 1:n_valid, :]
    h = cols + (prev - cols) * mu_ref[...]
    r, k, v = h[:, 0:aw], h[:, aw:2 * aw], h[:, 2 * aw:3 * aw]
    h_wa = h[:, 3 * aw:3 * aw + LANES]
    h_g = h[:, 3 * aw + LANES:3 * aw + 2 * LANES]
    z = dbase_ref[...] + jnp.dot(jnp.tanh(h_wa).astype(BF16), dup_ref[...], preferred_element_type=F32)
    w_raw = -_softplus(-z) - 0.5
    logw = -jnp.exp(w_raw)
    a = _sigmoid(ibase_ref[...] + jnp.dot(h_wa.astype(BF16), iup_ref[...], preferred_element_type=F32))
    g = jnp.dot(_sigmoid(h_g).astype(BF16), gup_ref[...], preferred_element_type=F32)
    kk = k * kks_ref[...]
    kk = kk / jnp.maximum(jnp.sqrt(_head_sum(kk * kk)), 1e-12)
    k = k * (1.0 + (a - 1.0) * kim_ref[...])
    bon = _head_sum(r * k * bonus_ref[...]) * v
    return r, logw, k, v, kk, kk * a, g, bon


def _tm_finish(y, bon, g, gnw_ref, gnb_ref):
    inv_n = 1.0 / A_HEAD_DIM
    y = y + bon
    mu = _head_sum(y) * inv_n
    d = y - mu
    var = _head_sum(d * d) * inv_n
    return (d * lax.rsqrt(var + GN_EPS) * gnw_ref[...] + gnb_ref[...]) * g


def _split_pair(x):
    lo = _pair_lo_mask()
    return jnp.concatenate([jnp.where(lo, x, 0.0), jnp.where(lo, 0.0, x)], axis=0)


def _mmb(a, b):
    return jnp.dot(a.astype(BF16), b.astype(BF16), preferred_element_type=F32)


def _mmb_nt(a, b):
    return _nt_dot(a.astype(BF16), b.astype(BF16))


def _timemix_chunk_kernel(cols_ref, mu_ref, dbase_ref, dup_ref, ibase_ref, iup_ref, gup_ref,
                          kks_ref, kim_ref, bonus_ref, gnw_ref, gnb_ref,
                          y_ref, wkv_ref, s_scr, carry_scr, *, bb):
    c = pl.program_id(1)
    cs_n = TM_CHUNK
    n2 = 2 * cs_n

    @pl.when(c == 0)
    def _init():
        s_scr[...] = jnp.zeros_like(s_scr)
        carry_scr[...] = jnp.zeros_like(carry_scr)

    prm = (mu_ref, dbase_ref, dup_ref, ibase_ref, iup_ref, gup_ref, kks_ref, kim_ref, bonus_ref)
    rr = lax.broadcasted_iota(jnp.int32, (n2, n2), 0)
    cc = lax.broadcasted_iota(jnp.int32, (n2, n2), 1)
    strict = rr > cc
    incl = rr >= cc
    eye = (rr == cc).astype(F32)
    tri = (lax.broadcasted_iota(jnp.int32, (cs_n, cs_n), 0)
           >= lax.broadcasted_iota(jnp.int32, (cs_n, cs_n), 1)).astype(BF16)

    pairs, tails = [], []
    for b in range(bb):
        r, logw, k, v, kk, ak, g, bon = _tm_prep(cols_ref[b], carry_scr, b, cs_n, prm)
        hi = logw.astype(BF16)
        r1 = logw - hi.astype(F32)
        mid = r1.astype(BF16)
        low = (r1 - mid.astype(F32)).astype(BF16)
        cum = (jnp.dot(tri, hi, preferred_element_type=F32) + jnp.dot(tri, mid, preferred_element_type=F32)
               + jnp.dot(tri, low, preferred_element_type=F32))
        e_prev, e_cur, e_inv = jnp.exp(cum - logw), jnp.exp(cum), jnp.exp(-cum)
        kt_all, rt_all, kb_all, ab_all = kk * e_prev, r * e_cur, k * e_inv, ak * e_inv
        tails.append((bon, g))
        for p in range(HEAD_PAIRS):
            sl = slice(p * LANES, (p + 1) * LANES)
            kt, rt, kb, ab, vv = (_split_pair(x[:, sl]) for x in (kt_all, rt_all, kb_all, ab_all, v))
            pairs.append(dict(kt=kt, rt=rt, kb=kb, ab=ab, vv=vv, decay=e_cur[cs_n - 1:cs_n, sl]))

    for q in pairs:
        gram = _mmb_nt(jnp.concatenate([q['kt'], q['rt']], axis=0),
                       jnp.concatenate([q['kb'], q['ab']], axis=0))
        q['akv'] = jnp.where(strict, gram[0:n2, 0:n2], 0.0)
        q['pw'] = jnp.where(strict, -gram[0:n2, n2:2 * n2], 0.0)
        q['brk'] = jnp.where(incl, gram[n2:2 * n2, 0:n2], 0.0)
        q['bra'] = jnp.where(incl, gram[n2:2 * n2, n2:2 * n2], 0.0)
        q['t'] = eye + q['pw']
    n = 1
    while 2 * n < cs_n:
        for q in pairs:
            q['pw'] = _mmb(q['pw'], q['pw'])
        for q in pairs:
            q['t'] = q['t'] + _mmb(q['t'], q['pw'])
        n *= 2
    for q in pairs:
        q['vt'] = q['vv'].T
        q['w1'] = _mmb(q['t'], q['kt'])
    for q in pairs:
        q['va'] = _mmb_nt(q['vt'], q['akv'])
    for i, q in enumerate(pairs):
        q['s'] = s_scr[i]
        q['ut'] = _mmb_nt(q['s'], q['w1']) + _mmb_nt(q['va'], q['t'])
    for i, q in enumerate(pairs):
        s_scr[i] = (q['s'] + _mmb(q['vt'], q['kb']) - _mmb(q['ut'], q['ab'])) * q['decay']
    for q in pairs:
        q['yt'] = _mmb_nt(q['s'], q['rt']) + _mmb_nt(q['vt'], q['brk']) - _mmb_nt(q['ut'], q['bra'])
    for b in range(bb):
        ys = []
        for p in range(HEAD_PAIRS):
            y2 = pairs[b * HEAD_PAIRS + p]['yt'].T
            ys.append(y2[0:cs_n, :] + y2[cs_n:n2, :])
        bon, g = tails[b]
        y_ref[b] = _tm_finish(jnp.concatenate(ys, axis=-1), bon, g, gnw_ref, gnb_ref)

    @pl.when(c == pl.num_programs(1) - 1)
    def _fin():
        wkv_ref[...] = s_scr[...].reshape(bb, HEAD_PAIRS, LANES, LANES)


def _timemix_chunk(cols, tm_params, *, bb):
    bsz, seq, _ = cols.shape
    tc = TM_CHUNK
    params = _tm_param_arrays(tm_params)
    y, wkv = pl.pallas_call(
        functools.partial(_timemix_chunk_kernel, bb=bb),
        grid=(bsz // bb, seq // tc),
        in_specs=[pl.BlockSpec((bb, tc, N_SHIFT), lambda i, c: (i, c, 0))] + [_full(p.shape) for p in params],
        out_specs=[pl.BlockSpec((bb, tc, A_WIDTH), lambda i, c: (i, c, 0)),
                   pl.BlockSpec((bb, HEAD_PAIRS, LANES, LANES), lambda i, c: (i, 0, 0, 0))],
        out_shape=[jax.ShapeDtypeStruct((bsz, seq, A_WIDTH), F32),
                   jax.ShapeDtypeStruct((bsz, HEAD_PAIRS, LANES, LANES), F32)],
        scratch_shapes=[pltpu.VMEM((bb * HEAD_PAIRS, LANES, LANES), F32), pltpu.VMEM((bb, 1, N_SHIFT), F32)],
        compiler_params=_cparams("arbitrary", "arbitrary"),
    )(cols, *params)
    wkv = wkv.reshape(bsz, HEAD_PAIRS, 2, A_HEAD_DIM, 2, A_HEAD_DIM)
    wkv = jnp.stack([wkv[:, :, 0, :, 0, :], wkv[:, :, 1, :, 1, :]], axis=2)
    return y, wkv.reshape(bsz, A_HEADS, A_HEAD_DIM, A_HEAD_DIM)


def _tm_param_arrays(tm_params):
    vec = lambda a: a.reshape(1, -1)
    mu, dbase, dup, ibase, iup, gup, kks, kim, bonus, gnw, gnb = tm_params
    zpad = jnp.zeros((LORA_DECAY, A_WIDTH), F32)
    dup_p = jnp.concatenate([dup, zpad], 0).astype(BF16)
    iup_p = jnp.concatenate([zpad, iup], 0).astype(BF16)
    return [vec(mu), vec(dbase), dup_p, vec(ibase), iup_p, gup.astype(BF16), vec(kks), vec(kim), vec(bonus),
            vec(gnw), vec(gnb)]


def _timemix_kernel(cols_ref, shift0_ref, wkv0_ref, mu_ref, dbase_ref, dup_ref, ibase_ref, iup_ref, gup_ref,
                    kks_ref, kim_ref, bonus_ref, gnw_ref, gnb_ref,
                    y_ref, wkv_ref,
                    s_scr, carry_scr, r_scr, w_scr, k_scr, kk_scr, ak_scr, g_scr, bon_scr, vpad_scr, vt_scr, yt_scr,
                    *, bb, tc, n_steps):
    c = pl.program_id(1)

    @pl.when(c == 0)
    def _init():
        s_scr[...] = wkv0_ref[...].reshape(bb * HEAD_PAIRS, A_HEAD_DIM, LANES)
        carry_scr[...] = shift0_ref[...]
        vpad_scr[...] = jnp.zeros_like(vpad_scr)

    prm = (mu_ref, dbase_ref, dup_ref, ibase_ref, iup_ref, gup_ref, kks_ref, kim_ref, bonus_ref)
    for b in range(bb):
        r, logw, k, v, kk, ak, g, bon = _tm_prep(cols_ref[b], carry_scr, b, n_steps, prm)
        for ref, val in ((r_scr, r), (w_scr, jnp.exp(logw)), (k_scr, k), (kk_scr, kk), (ak_scr, ak)):
            for p in range(HEAD_PAIRS):
                ref[b * HEAD_PAIRS + p] = val[:, p * LANES:(p + 1) * LANES]
        g_scr[b] = g
        bon_scr[b] = bon
        if tc == LANES:
            vt_scr[b] = v.T
        else:
            vpad_scr[b, 0:tc, :] = v
            vt_scr[b] = vpad_scr[b].T
    yt_scr[...] = jnp.zeros_like(yt_scr)

    lane = lax.broadcasted_iota(jnp.int32, (1, LANES), 1)
    lo = lane < A_HEAD_DIM
    hd = A_HEAD_DIM

    def seg_sum(x):
        s_lo = jnp.sum(jnp.where(lo, x, 0.0), axis=-1, keepdims=True)
        s_hi = jnp.sum(jnp.where(lo, 0.0, x), axis=-1, keepdims=True)
        return s_lo, s_hi

    n_sub = min(SUBLANES, n_steps)

    def block(i, carry):
        base = pl.multiple_of(i * SUBLANES, SUBLANES)
        n_pairs = bb * HEAD_PAIRS
        ss = [s_scr[idx] for idx in range(n_pairs)]
        yas = [yt_scr[idx // HEAD_PAIRS, (idx % HEAD_PAIRS) * LANES:(idx % HEAD_PAIRS) * LANES + hd, :]
               for idx in range(n_pairs)]
        ybs = [yt_scr[idx // HEAD_PAIRS, (idx % HEAD_PAIRS) * LANES + hd:(idx % HEAD_PAIRS + 1) * LANES, :]
               for idx in range(n_pairs)]
        for j in range(n_sub):
            tmask = lane == base + j
            rows = [[jnp.broadcast_to(ref[idx, pl.ds(base, SUBLANES), :][j:j + 1, :], (hd, LANES))
                     for ref in (r_scr, w_scr, k_scr, kk_scr, ak_scr)] for idx in range(n_pairs)]
            v_cols = []
            for idx in range(n_pairs):
                b, p = divmod(idx, HEAD_PAIRS)
                ra, rb = p * LANES, p * LANES + hd
                v_a = jnp.sum(jnp.where(tmask, vt_scr[b, ra:ra + hd, :], 0.0), axis=-1, keepdims=True)
                v_b = jnp.sum(jnp.where(tmask, vt_scr[b, rb:rb + hd, :], 0.0), axis=-1, keepdims=True)
                v_cols.append(jnp.where(lo, v_a, v_b))
            us = [seg_sum(ss[idx] * rows[idx][3]) for idx in range(n_pairs)]
            for idx in range(n_pairs):
                r_t, w_t, k_t, kk_t, ak_t = rows[idx]
                u_a, u_b = us[idx]
                ss[idx] = ss[idx] * w_t - jnp.where(lo, u_a, u_b) * ak_t + v_cols[idx] * k_t
            ys = [seg_sum(ss[idx] * rows[idx][0]) for idx in range(n_pairs)]
            for idx in range(n_pairs):
                yas[idx] = jnp.where(tmask, ys[idx][0], yas[idx])
                ybs[idx] = jnp.where(tmask, ys[idx][1], ybs[idx])
        for idx in range(n_pairs):
            b, p = divmod(idx, HEAD_PAIRS)
            s_scr[idx] = ss[idx]
            yt_scr[b, p * LANES:p * LANES + hd, :] = yas[idx]
            yt_scr[b, p * LANES + hd:(p + 1) * LANES, :] = ybs[idx]
        return carry

    lax.fori_loop(0, n_steps // n_sub, block, 0)

    for b in range(bb):
        y_ref[b] = _tm_finish(yt_scr[b].T[0:tc, :], bon_scr[b], g_scr[b], gnw_ref, gnb_ref)

    @pl.when(c == pl.num_programs(1) - 1)
    def _fin():
        wkv_ref[...] = s_scr[...].reshape(bb, HEAD_PAIRS, A_HEAD_DIM, LANES)


def _pack_pairs(wkv):
    b = wkv.shape[0]
    x = wkv.reshape(b, HEAD_PAIRS, 2, A_HEAD_DIM, A_HEAD_DIM)
    return jnp.swapaxes(x, 2, 3).reshape(b, HEAD_PAIRS, A_HEAD_DIM, LANES)


def _unpack_pairs(x):
    b = x.shape[0]
    x = x.reshape(b, HEAD_PAIRS, A_HEAD_DIM, 2, A_HEAD_DIM)
    return jnp.swapaxes(x, 2, 3).reshape(b, A_HEADS, A_HEAD_DIM, A_HEAD_DIM)


def _timemix(cols, shift0, wkv0, tm_params, *, bb, tc, n_steps):
    bsz, seq, _ = cols.shape
    nb, nc = bsz // bb, seq // tc
    params = _tm_param_arrays(tm_params)
    scr = lambda *s: pltpu.VMEM(s, F32)
    y, wkv = pl.pallas_call(
        functools.partial(_timemix_kernel, bb=bb, tc=tc, n_steps=n_steps),
        grid=(nb, nc),
        in_specs=[pl.BlockSpec((bb, tc, N_SHIFT), lambda i, c: (i, c, 0)),
                  pl.BlockSpec((bb, 1, N_SHIFT), lambda i, c: (i, 0, 0)),
                  pl.BlockSpec((bb, HEAD_PAIRS, A_HEAD_DIM, LANES), lambda i, c: (i, 0, 0, 0))]
                 + [_full(p.shape) for p in params],
        out_specs=[pl.BlockSpec((bb, tc, A_WIDTH), lambda i, c: (i, c, 0)),
                   pl.BlockSpec((bb, HEAD_PAIRS, A_HEAD_DIM, LANES), lambda i, c: (i, 0, 0, 0))],
        out_shape=[jax.ShapeDtypeStruct((bsz, seq, A_WIDTH), F32),
                   jax.ShapeDtypeStruct((bsz, HEAD_PAIRS, A_HEAD_DIM, LANES), F32)],
        scratch_shapes=[scr(bb * HEAD_PAIRS, A_HEAD_DIM, LANES), scr(bb, 1, N_SHIFT)]
                       + [scr(bb * HEAD_PAIRS, tc, LANES)] * 5 + [scr(bb, tc, A_WIDTH)] * 2
                       + [scr(bb, LANES, A_WIDTH), scr(bb, A_WIDTH, LANES), scr(bb, A_WIDTH, LANES)],
        compiler_params=_cparams("arbitrary", "arbitrary"),
    )(cols, shift0.reshape(bsz, 1, N_SHIFT), _pack_pairs(wkv0), *params)
    return y, _unpack_pairs(wkv)


def _bucket(n):
    max_exact = NUM_BUCKETS // 2
    nf = jnp.maximum(n, 1).astype(F32)
    large = max_exact + jnp.floor(jnp.log(nf / max_exact) / math.log(MAX_DISTANCE / max_exact)
                                  * (NUM_BUCKETS - max_exact)).astype(jnp.int32)
    return jnp.where(n < max_exact, n, jnp.minimum(large, NUM_BUCKETS - 1))


def _prompt_bias_kernel(table_ref, out_ref):
    h = pl.program_id(0)
    t = ATT_T
    rr = lax.broadcasted_iota(jnp.int32, (t, t), 0)
    cc = lax.broadcasted_iota(jnp.int32, (t, t), 1)
    far = table_ref[NUM_BUCKETS - 1, h]
    for i in range(2):
        d = (1 - i) * t + rr - cc
        bkt = _bucket(jnp.maximum(d, 0))
        bias = jnp.zeros((t, t), F32)
        for bk in range(NUM_BUCKETS):
            bias = jnp.where(bkt == bk, table_ref[bk, h], bias)
        out_ref[0, i] = jnp.where(d >= 0, (bias - far) * LOG2E, NEG_BIG)


def _prompt_bias(table):
    t = ATT_T
    return pl.pallas_call(
        _prompt_bias_kernel,
        grid=(B_HEADS,),
        in_specs=[pl.BlockSpec(memory_space=pltpu.SMEM)],
        out_specs=pl.BlockSpec((1, 2, t, t), lambda h: (h, 0, 0, 0)),
        out_shape=jax.ShapeDtypeStruct((B_HEADS, 2, t, t), F32),
        compiler_params=_cparams("arbitrary"),
    )(table)


def _sample_rows(n_new):
    return n_new * 2 * B_HEADS


def _sample_bias_kernel(table_ref, out_ref, *, n_new, page):
    nr = _sample_rows(n_new)
    rr = lax.broadcasted_iota(jnp.int32, (nr, LANES), 0)
    cc = lax.broadcasted_iota(jnp.int32, (nr, LANES), 1)
    tok = rr % n_new
    head = rr // (2 * n_new)

    def lookup(d):
        bkt = _bucket(jnp.maximum(d, 0))
        bias = jnp.zeros((nr, LANES), F32)
        for hh in range(B_HEADS):
            for bk in range(NUM_BUCKETS):
                bias = jnp.where((bkt == bk) & (head == hh), table_ref[bk, hh], bias)
        return bias

    far = lookup(jnp.full((nr, LANES), MAX_DISTANCE, jnp.int32))
    out_ref[0] = far * LOG2E
    out_ref[1] = lookup(tok + page - cc) * LOG2E
    d_self = tok - cc
    out_ref[2] = jnp.where((d_self >= 0) & (cc < n_new), lookup(d_self) * LOG2E, NEG_BIG)


def _sample_bias(table, n_new, page):
    nr = _sample_rows(n_new)
    return pl.pallas_call(
        functools.partial(_sample_bias_kernel, n_new=n_new, page=page),
        in_specs=[pl.BlockSpec(memory_space=pltpu.SMEM)],
        out_specs=pl.BlockSpec(memory_space=pltpu.VMEM),
        out_shape=jax.ShapeDtypeStruct((3, nr, LANES), F32),
    )(table)


def _lambda(lq1_ref, lk1_ref, lq2_ref, lk2_ref, lam_init):
    s1 = jnp.sum(lq1_ref[...] * lk1_ref[...], axis=-1, keepdims=True)
    s2 = jnp.sum(lq2_ref[...] * lk2_ref[...], axis=-1, keepdims=True)
    return jnp.exp(s1) - jnp.exp(s2) + lam_init


def _nt_dot(a, b):
    return lax.dot_general(a, b, (((1,), (1,)), ((), ())), preferred_element_type=F32)


def _attn_prompt_kernel(q_ref, k_ref, v_ref, bias_ref, lq1_ref, lk1_ref, lq2_ref, lk2_ref, subln_ref, o_ref,
                        acc_scr, *, lam_init):
    i = pl.program_id(2)
    t = ATT_T
    nh = ATT_HEADS
    lane = lax.broadcasted_iota(jnp.int32, (1, LANES), 1)
    first = lane < B_HEAD_DIM
    chains = [(hh, c) for hh in range(nh) for c in range(2)]
    qs = []
    for hh, c in chains:
        q = q_ref[0, :, hh * B_V_DIM:(hh + 1) * B_V_DIM]
        qs.append((jnp.where(first, q, 0.0) if c == 0 else jnp.where(first, 0.0, q)).astype(BF16))
    acc_scr[...] = jnp.zeros_like(acc_scr)

    def tile(j, carry, near):
        ms, ls = carry
        off = pl.multiple_of(j * t, t)
        kts = [k_ref[0, pl.ds(off, t), hh * B_V_DIM:(hh + 1) * B_V_DIM] for hh in range(nh)]
        vts = [v_ref[0, pl.ds(off, t), hh * B_V_DIM:(hh + 1) * B_V_DIM] for hh in range(nh)]
        ss = [_nt_dot(qs[n], kts[hh]) for n, (hh, _) in enumerate(chains)]
        if near:
            biases = [bias_ref[hh, j - i + 1] for hh in range(nh)]
            ss = [s + biases[hh] for s, (hh, _) in zip(ss, chains)]
        m_new = [jnp.maximum(m, jnp.max(s, axis=-1, keepdims=True)) for m, s in zip(ms, ss)]
        ps = [jnp.exp2(s - m) for s, m in zip(ss, m_new)]
        alphas = [jnp.exp2(m - mn) for m, mn in zip(ms, m_new)]
        ls = tuple(a * l + jnp.sum(p, axis=-1, keepdims=True) for a, l, p in zip(alphas, ls, ps))
        for n, (hh, _) in enumerate(chains):
            acc_scr[n] = alphas[n] * acc_scr[n] + jnp.dot(ps[n].astype(BF16), vts[hh],
                                                          preferred_element_type=F32)
        return tuple(m_new), ls

    m0 = tuple(jnp.full((t, 1), NEG_BIG, F32) for _ in chains)
    l0 = tuple(jnp.zeros((t, 1), F32) for _ in chains)
    n_far = jnp.maximum(i - 1, 0)
    carry = lax.fori_loop(0, n_far, functools.partial(tile, near=False), (m0, l0))
    _, ls = lax.fori_loop(n_far, i + 1, functools.partial(tile, near=True), carry)
    lam = _lambda(lq1_ref, lk1_ref, lq2_ref, lk2_ref, lam_init)
    for hh in range(nh):
        o = acc_scr[2 * hh] * (1.0 / ls[2 * hh]) - acc_scr[2 * hh + 1] * (lam / ls[2 * hh + 1])
        ms = jnp.sum(o * o, axis=-1, keepdims=True) * (1.0 / B_V_DIM)
        o_ref[0, :, hh * B_V_DIM:(hh + 1) * B_V_DIM] = (o * lax.rsqrt(ms + RMS_EPS) * subln_ref[...]
                                                        * (1.0 - lam_init))


def _attn_prompt(q, k16, v16, bias, lam_params, subln, lam_init):
    bsz, seq, _ = q.shape
    t = ATT_T
    nh = ATT_HEADS
    w = nh * B_V_DIM
    lam_specs = [_full((1, B_HEAD_DIM))] * 4
    return pl.pallas_call(
        functools.partial(_attn_prompt_kernel, lam_init=lam_init),
        grid=(bsz, B_HEADS // nh, seq // t),
        in_specs=[pl.BlockSpec((1, t, w), lambda b, h, i: (b, i, h)),
                  pl.BlockSpec((1, seq, w), lambda b, h, i: (b, 0, h)),
                  pl.BlockSpec((1, seq, w), lambda b, h, i: (b, 0, h)),
                  pl.BlockSpec((nh, 2, t, t), lambda b, h, i: (h, 0, 0, 0))]
                 + lam_specs + [_full((1, B_V_DIM))],
        out_specs=pl.BlockSpec((1, t, w), lambda b, h, i: (b, i, h)),
        out_shape=jax.ShapeDtypeStruct((bsz, seq, B_WIDTH), F32),
        scratch_shapes=[pltpu.VMEM((2 * nh, t, B_V_DIM), F32)],
        compiler_params=_cparams("arbitrary", "arbitrary", "arbitrary"),
    )(q, k16, v16, bias, *lam_params, subln)


def _attn_sample_kernel(pt_ref, q_ref, kn_ref, vn_ref, sb_ref, lq1_ref, lk1_ref, lq2_ref, lk2_ref, subln_ref,
                        *rest, n_new, lam_init):
    pg = PAGES_PER_STEP
    k_refs, v_refs = rest[:pg], rest[pg:2 * pg]
    o_ref = rest[2 * pg]
    qb_scr, m_scr, l_scr, acc_scr, kn_scr, vn_scr = rest[2 * pg + 1:]
    del pt_ref
    g = pl.program_id(1)
    ng = pl.num_programs(1)
    nr = _sample_rows(n_new)
    rph = 2 * n_new
    page = kn_scr.shape[0]

    @pl.when(g == 0)
    def _init():
        rr = lax.broadcasted_iota(jnp.int32, (rph, B_WIDTH), 0)
        cc = lax.broadcasted_iota(jnp.int32, (rph, B_WIDTH), 1)
        q = jnp.where(rr < n_new, q_ref[0], 0.0)
        q2 = q + pltpu.roll(q, n_new, axis=0)
        for h in range(B_HEADS):
            sel = (cc // B_HEAD_DIM) == 2 * h + rr // n_new
            qb_scr[h * rph:(h + 1) * rph, :] = jnp.where(sel, q2, 0.0)
        m_scr[...] = jnp.full_like(m_scr, NEG_BIG)
        l_scr[...] = jnp.zeros_like(l_scr)
        acc_scr[...] = jnp.zeros_like(acc_scr)
        kn_scr[...] = jnp.zeros_like(kn_scr)
        vn_scr[...] = jnp.zeros_like(vn_scr)
        kn_scr[0:SAMPLE_T_PAD, :] = kn_ref[0]
        vn_scr[0:SAMPLE_T_PAD, :] = vn_ref[0]

    def update(s_list, v_list):
        m = m_scr[...]
        m_new = m
        for s in s_list:
            m_new = jnp.maximum(m_new, jnp.max(s, axis=-1, keepdims=True))
        alpha = jnp.exp2(m - m_new)
        l = alpha * l_scr[...]
        acc = [alpha[h * rph:(h + 1) * rph] * acc_scr[h * rph:(h + 1) * rph, :] for h in range(B_HEADS)]
        for s, v_heads in zip(s_list, v_list):
            p = jnp.exp2(s - m_new)
            l = l + jnp.sum(p, axis=-1, keepdims=True)
            for h in range(B_HEADS):
                acc[h] = acc[h] + jnp.dot(p[h * rph:(h + 1) * rph, :].astype(BF16), v_heads[h],
                                          preferred_element_type=F32)
        m_scr[...] = m_new
        l_scr[...] = l
        for h in range(B_HEADS):
            acc_scr[h * rph:(h + 1) * rph, :] = acc[h]

    qb = qb_scr[...].astype(BF16)
    far = sb_ref[0]
    s_list, v_list = [], []
    for i in range(pg):
        bias = far
        if i == pg - 1:
            bias = sb_ref[jnp.where(g == ng - 1, 1, 0)]
        s_list.append(jnp.dot(qb, k_refs[i][0].astype(BF16), preferred_element_type=F32) + bias)
        v_list.append([v_refs[i][0, pl.ds(h, page, stride=B_HEADS), :].astype(BF16) for h in range(B_HEADS)])
    update(s_list, v_list)

    @pl.when(g == ng - 1)
    def _fin():
        vn = vn_scr[...].astype(BF16)
        update([_nt_dot(qb, kn_scr[...].astype(BF16)) + sb_ref[2]],
               [[vn[:, h * B_V_DIM:(h + 1) * B_V_DIM] for h in range(B_HEADS)]])
        lam = _lambda(lq1_ref, lk1_ref, lq2_ref, lk2_ref, lam_init)
        rr = lax.broadcasted_iota(jnp.int32, (nr, 1), 0)
        coef = jnp.where(rr % rph < n_new, 1.0, -lam) / l_scr[...]
        w = acc_scr[...] * coef
        parts = []
        for h in range(B_HEADS):
            wh = w[h * rph:(h + 1) * rph, :]
            oh = wh + pltpu.roll(wh, n_new, axis=0)
            ms = jnp.sum(oh * oh, axis=-1, keepdims=True) * (1.0 / B_V_DIM)
            parts.append(oh * lax.rsqrt(ms + RMS_EPS) * subln_ref[...] * (1.0 - lam_init))
        o_ref[0] = jnp.concatenate(parts, axis=-1)


def _attn_sample(q, k_new, v_new, cache_k, cache_v, page_table, sbias, lam_params, subln, lam_init, n_new):
    bsz = q.shape[0]
    n_pages = page_table.shape[1]
    page = cache_k.shape[2]
    pg = PAGES_PER_STEP
    nr = _sample_rows(n_new)
    assert 2 * n_new == SAMPLE_T_PAD == SUBLANES
    tok_spec = pl.BlockSpec((1, SAMPLE_T_PAD, B_WIDTH), lambda b, g, pt: (b, 0, 0))

    def page_spec(i):
        return pl.BlockSpec((1, B_WIDTH, page), lambda b, g, pt: (pt[b, g * pg + i], 0, 0))

    const = lambda shape: pl.BlockSpec(shape, lambda b, g, pt: (0,) * len(shape))
    grid_spec = pltpu.PrefetchScalarGridSpec(
        num_scalar_prefetch=1,
        grid=(bsz, n_pages // pg),
        in_specs=[tok_spec, tok_spec, tok_spec, const((3, nr, LANES))]
                 + [const((1, B_HEAD_DIM))] * 4 + [const((1, B_V_DIM))]
                 + [page_spec(i) for i in range(pg)] * 2,
        out_specs=tok_spec,
        scratch_shapes=[pltpu.VMEM((nr, B_WIDTH), F32), pltpu.VMEM((nr, 1), F32), pltpu.VMEM((nr, 1), F32),
                        pltpu.VMEM((nr, B_V_DIM), F32), pltpu.VMEM((page, B_WIDTH), F32),
                        pltpu.VMEM((page, B_WIDTH), F32)],
    )
    return pl.pallas_call(
        functools.partial(_attn_sample_kernel, n_new=n_new, lam_init=lam_init),
        grid_spec=grid_spec,
        out_shape=jax.ShapeDtypeStruct((bsz, SAMPLE_T_PAD, B_WIDTH), F32),
        compiler_params=_cparams("arbitrary", "arbitrary"),
    )(page_table, q, k_new, v_new, sbias, *lam_params, subln, *([cache_k] * pg), *([cache_v] * pg))


def _layer_norm(z, w, b):
    inv_d = 1.0 / D_MODEL
    mu = jnp.sum(z, axis=-1, keepdims=True) * inv_d
    d = z - mu
    var = jnp.sum(d * d, axis=-1, keepdims=True) * inv_d
    return d * lax.rsqrt(var + LN_EPS) * w + b


def _merge_kernel(x_ref, ya_ref, ob_ref, gates_ref, wa_ref, wb_ref, wo_ref, lnw_ref, lnb_ref, o_ref):
    ga = _sigmoid(gates_ref[:, 0:D_MODEL])
    gb = _sigmoid(gates_ref[:, D_MODEL:2 * D_MODEL])
    merged = (ga * jnp.dot(ya_ref[...].astype(BF16), wa_ref[...], preferred_element_type=F32)
              + gb * jnp.dot(ob_ref[...].astype(BF16), wb_ref[...], preferred_element_type=F32))
    mix = jnp.dot(merged.astype(BF16), wo_ref[...], preferred_element_type=F32)
    o_ref[...] = _layer_norm(DEEPNORM_ALPHA * x_ref[...] + mix, lnw_ref[...], lnb_ref[...])


def _merge(x2d, ya, ob, gates, wa, wb, wo, lnw, lnb):
    m = x2d.shape[0]
    tm = min(MERGE_TM, m)
    row = lambda w: pl.BlockSpec((tm, w), lambda i: (i, 0))
    return pl.pallas_call(
        _merge_kernel,
        grid=(m // tm,),
        in_specs=[row(D_MODEL), row(A_WIDTH), row(B_WIDTH), row(2 * D_MODEL),
                  _full(wa.shape), _full(wb.shape), _full(wo.shape), _full((1, D_MODEL)), _full((1, D_MODEL))],
        out_specs=row(D_MODEL),
        out_shape=jax.ShapeDtypeStruct((m, D_MODEL), F32),
        compiler_params=_cparams("arbitrary"),
    )(x2d, ya, ob, gates, wa, wb, wo, lnw.reshape(1, -1), lnb.reshape(1, -1))


def _gelu(x):
    return 0.5 * x * (1.0 + lax.erf(x * (2.0 ** -0.5)))


def _ffn_prompt_kernel(x_ref, wu_ref, wg_ref, cwu_ref, cwg_ref, cbu_ref, cbg_ref, wd_ref, lnw_ref, lnb_ref,
                       o_ref, tu_ref, tg_ref, acc_scr, cu_scr, cg_scr):
    i, j = pl.program_id(1), pl.program_id(2)
    tm = x_ref.shape[1]
    rows = lax.broadcasted_iota(jnp.int32, (tm, 1), 0)
    xb = x_ref[0].astype(BF16)

    @pl.when(i == 0)
    def _reset():
        cu_scr[j] = jnp.zeros((SUBLANES, FFN_TF), F32)
        cg_scr[j] = jnp.zeros((SUBLANES, FFN_TF), F32)

    def conv(w_ref, cw_ref, cb_ref, carry_scr, tail_ref):
        h = jnp.dot(xb, w_ref[...], preferred_element_type=F32)
        carry = carry_scr[j]
        c6, c7 = carry[SUBLANES - 2:SUBLANES - 1, :], carry[SUBLANES - 1:SUBLANES, :]
        h1 = jnp.where(rows == 0, c7, pltpu.roll(h, 1, axis=0))
        h2 = jnp.where(rows == 0, c6, jnp.where(rows == 1, c7, pltpu.roll(h, 2, axis=0)))
        tail = h[tm - SUBLANES:tm, :]
        carry_scr[j] = tail
        tail_ref[0, 0] = tail
        return cb_ref[...] + (cw_ref[0:1, :] * h2 + cw_ref[1:2, :] * h1 + cw_ref[2:3, :] * h)

    u = conv(wu_ref, cwu_ref, cbu_ref, cu_scr, tu_ref)
    g = conv(wg_ref, cwg_ref, cbg_ref, cg_scr, tg_ref)
    part = jnp.dot((_gelu(g) * u).astype(BF16), wd_ref[...], preferred_element_type=F32)

    @pl.when(j == 0)
    def _first():
        acc_scr[...] = part

    @pl.when(j > 0)
    def _rest():
        acc_scr[...] += part

    @pl.when(j == pl.num_programs(2) - 1)
    def _fin():
        o_ref[0] = _layer_norm(DEEPNORM_ALPHA * x_ref[0] + acc_scr[...], lnw_ref[...], lnb_ref[...])


def _ffn_prompt(x, w_up, conv_w, conv_b, w_down, lnw, lnb):
    bsz, seq, _ = x.shape
    tm, tf = FFN_TM, FFN_TF
    nf = FFN_DIM // tf
    nt = seq // tm
    conv_b = conv_b.reshape(1, -1)
    ucol = lambda r: pl.BlockSpec((r, tf), lambda b, i, j: (0, j))
    gcol = lambda r: pl.BlockSpec((r, tf), lambda b, i, j: (0, nf + j))
    tail = pl.BlockSpec((1, 1, SUBLANES, tf), lambda b, i, j: (b, i, 0, j))
    return pl.pallas_call(
        _ffn_prompt_kernel,
        grid=(bsz, nt, nf),
        in_specs=[pl.BlockSpec((1, tm, D_MODEL), lambda b, i, j: (b, i, 0)),
                  ucol(D_MODEL), gcol(D_MODEL), ucol(CONV_WIDTH), gcol(CONV_WIDTH), ucol(1), gcol(1),
                  pl.BlockSpec((tf, D_MODEL), lambda b, i, j: (j, 0)),
                  pl.BlockSpec((1, D_MODEL), lambda b, i, j: (0, 0)),
                  pl.BlockSpec((1, D_MODEL), lambda b, i, j: (0, 0))],
        out_specs=[pl.BlockSpec((1, tm, D_MODEL), lambda b, i, j: (b, i, 0)), tail, tail],
        out_shape=[jax.ShapeDtypeStruct((bsz, seq, D_MODEL), F32),
                   jax.ShapeDtypeStruct((bsz, nt, SUBLANES, FFN_DIM), F32),
                   jax.ShapeDtypeStruct((bsz, nt, SUBLANES, FFN_DIM), F32)],
        scratch_shapes=[pltpu.VMEM((tm, D_MODEL), F32), pltpu.VMEM((nf, SUBLANES, tf), F32),
                        pltpu.VMEM((nf, SUBLANES, tf), F32)],
        compiler_params=_cparams("arbitrary", "arbitrary", "arbitrary"),
    )(x, w_up, w_up, conv_w, conv_w, conv_b, conv_b, w_down, lnw.reshape(1, -1), lnb.reshape(1, -1))


def _ffn_sample_kernel(x_ref, bu_ref, bg_ref, wu_ref, wg_ref, cwu_ref, cwg_ref, cbu_ref, cbg_ref, wd_ref,
                       lnw_ref, lnb_ref, o_ref, tu_ref, tg_ref, acc_scr, *, n_new, bsz):
    j = pl.program_id(0)
    m = n_new * bsz
    nbuf = (CONV_WIDTH - 1) * bsz
    xb = x_ref[...].astype(BF16)

    def conv(w_ref, cw_ref, cb_ref, buf_ref, tail_ref):
        h = jnp.dot(xb, w_ref[...], preferred_element_type=F32)
        hp = jnp.concatenate([buf_ref[...], h], axis=0)
        tail_ref[...] = hp[m:m + nbuf, :]
        return cb_ref[...] + (cw_ref[0:1, :] * hp[0:m, :] + cw_ref[1:2, :] * hp[bsz:bsz + m, :]
                              + cw_ref[2:3, :] * hp[2 * bsz:2 * bsz + m, :])

    u = conv(wu_ref, cwu_ref, cbu_ref, bu_ref, tu_ref)
    g = conv(wg_ref, cwg_ref, cbg_ref, bg_ref, tg_ref)
    part = jnp.dot((_gelu(g) * u).astype(BF16), wd_ref[...], preferred_element_type=F32)

    @pl.when(j == 0)
    def _first():
        acc_scr[...] = part

    @pl.when(j > 0)
    def _rest():
        acc_scr[...] += part

    @pl.when(j == pl.num_programs(0) - 1)
    def _fin():
        o_ref[...] = _layer_norm(DEEPNORM_ALPHA * x_ref[...] + acc_scr[...], lnw_ref[...], lnb_ref[...])


def _ffn_sample(x_pm, buf_pm, w_up, conv_w, conv_b, w_down, lnw, lnb, n_new, bsz):
    m = n_new * bsz
    nbuf = (CONV_WIDTH - 1) * bsz
    tf = FFN_TF
    nf = FFN_DIM // tf
    conv_b = conv_b.reshape(1, -1)
    ucol = lambda r: pl.BlockSpec((r, tf), lambda j: (0, j))
    gcol = lambda r: pl.BlockSpec((r, tf), lambda j: (0, nf + j))
    return pl.pallas_call(
        functools.partial(_ffn_sample_kernel, n_new=n_new, bsz=bsz),
        grid=(nf,),
        in_specs=[_full((m, D_MODEL)), ucol(nbuf), gcol(nbuf), ucol(D_MODEL), gcol(D_MODEL),
                  ucol(CONV_WIDTH), gcol(CONV_WIDTH), ucol(1), gcol(1),
                  pl.BlockSpec((tf, D_MODEL), lambda j: (j, 0)), _full((1, D_MODEL)), _full((1, D_MODEL))],
        out_specs=[_full((m, D_MODEL)), ucol(nbuf), ucol(nbuf)],
        out_shape=[jax.ShapeDtypeStruct((m, D_MODEL), F32), jax.ShapeDtypeStruct((nbuf, FFN_DIM), F32),
                   jax.ShapeDtypeStruct((nbuf, FFN_DIM), F32)],
        scratch_shapes=[pltpu.VMEM((m, D_MODEL), F32)],
        compiler_params=_cparams("arbitrary"),
    )(x_pm, buf_pm, buf_pm, w_up, w_up, conv_w, conv_w, conv_b, conv_b, w_down,
      lnw.reshape(1, -1), lnb.reshape(1, -1))


def kernel(x_prompt, x_sample, cache_k, cache_v, state_wkv, state_shift, state_conv, page_table, rel_bias_table, w_in, mu_shift, decay_base, decay_up, iclr_base, iclr_up, gate_up, kk_scale, k_iclr_mix, bonus_rk, gn_w, gn_b, lam_q1, lam_k1, lam_q2, lam_k2, subln_w, w_branch_a, w_branch_b, w_out, ln1_w, ln1_b, w_up, conv_w, conv_b, w_down, ln2_w, ln2_b):
    assert w_in.shape[0] == DEPTH == 1
    l = 0
    lam_init = 0.8 - 0.6 * math.exp(-0.3 * l)
    bp, seq, _ = x_prompt.shape
    bs, n_new, _ = x_sample.shape
    page = cache_k.shape[2]
    assert page == LANES and page >= MAX_DISTANCE and ATT_T >= MAX_DISTANCE and n_new <= SAMPLE_T_PAD

    w_in_b = w_in[l].astype(BF16)
    wa, wb, wo = w_branch_a[l].astype(BF16), w_branch_b[l].astype(BF16), w_out[l].astype(BF16)
    w_up_b, w_down_b = w_up[l].astype(BF16), w_down[l].astype(BF16)
    tm_params = (mu_shift[l], decay_base[l], decay_up[l], iclr_base[l], iclr_up[l], gate_up[l], kk_scale[l],
                 k_iclr_mix[l], bonus_rk[l].reshape(-1), gn_w[l], gn_b[l])
    lam_params = [p[l].reshape(1, -1) for p in (lam_q1, lam_k1, lam_q2, lam_k2)]
    subln = subln_w[l].reshape(1, -1)

    mp = bp * seq
    xp2 = x_prompt.reshape(mp, D_MODEL)
    cols, q, k, v, gates, k16, v16 = _proj(xp2, w_in_b, True)
    cols3 = cols.reshape(bp, seq, N_SHIFT)
    ya, wkv_p = _timemix_chunk(cols3, tm_params, bb=bp)
    r3 = lambda a: a.reshape(bp, seq, B_WIDTH)
    ob = _attn_prompt(r3(q), r3(k16), r3(v16), _prompt_bias(rel_bias_table), lam_params, subln, lam_init)
    x1 = _merge(xp2, ya.reshape(mp, A_WIDTH), ob.reshape(mp, B_WIDTH), gates, wa, wb, wo, ln1_w[l], ln1_b[l])
    y_p, tu, tg = _ffn_prompt(x1.reshape(bp, seq, D_MODEL), w_up_b, conv_w[l], conv_b[l], w_down_b,
                              ln2_w[l], ln2_b[l])
    nbuf = CONV_WIDTH - 1
    k_prompt = k.reshape(1, bp, seq, B_HEADS, 2, B_HEAD_DIM)
    v_prompt = v.reshape(1, bp, seq, B_HEADS, B_V_DIM)
    shift_prompt = cols3[:, seq - 1][None]
    conv_prompt = jnp.concatenate([tu[:, -1, SUBLANES - nbuf:], tg[:, -1, SUBLANES - nbuf:]], axis=-1)[None]

    tp = SAMPLE_T_PAD
    xs8 = jnp.pad(x_sample, ((0, 0), (0, tp - n_new), (0, 0))).reshape(bs * tp, D_MODEL)
    cols, q, k, v, gates = _proj(xs8, w_in_b, False)
    cols3 = cols.reshape(bs, tp, N_SHIFT)
    ya, wkv_s = _timemix(cols3, state_shift[l], state_wkv[l], tm_params, bb=4, tc=tp, n_steps=n_new)
    r3 = lambda a: a.reshape(bs, tp, B_WIDTH)
    n_phys = cache_k.shape[1]
    cache_kt = jnp.transpose(cache_k[l], (0, 2, 3, 4, 1)).reshape(n_phys, B_WIDTH, page)
    cache_v2 = cache_v[l].reshape(n_phys, page * B_HEADS, B_V_DIM)
    ob = _attn_sample(r3(q), r3(k), r3(v), cache_kt, cache_v2, page_table,
                      _sample_bias(rel_bias_table, n_new, page), lam_params, subln, lam_init, n_new)
    x1 = _merge(xs8, ya.reshape(bs * tp, A_WIDTH), ob.reshape(bs * tp, B_WIDTH), gates, wa, wb, wo,
                ln1_w[l], ln1_b[l])
    x1_pm = jnp.swapaxes(x1.reshape(bs, tp, D_MODEL)[:, :n_new], 0, 1).reshape(n_new * bs, D_MODEL)
    buf_pm = jnp.swapaxes(state_conv[l], 0, 1).reshape(nbuf * bs, 2 * FFN_DIM)
    y_pm, tu, tg = _ffn_sample(x1_pm, buf_pm, w_up_b, conv_w[l], conv_b[l], w_down_b, ln2_w[l], ln2_b[l],
                               n_new, bs)
    y_s = jnp.swapaxes(y_pm.reshape(n_new, bs, D_MODEL), 0, 1)
    conv_sample = jnp.swapaxes(jnp.concatenate([tu, tg], axis=-1).reshape(nbuf, bs, 2 * FFN_DIM), 0, 1)[None]
    k_sample = k.reshape(bs, tp, B_HEADS, 2, B_HEAD_DIM)[:, :n_new][None]
    v_sample = v.reshape(bs, tp, B_HEADS, B_V_DIM)[:, :n_new][None]
    shift_sample = cols3[:, n_new - 1][None]

    return (y_p, y_s, k_prompt, v_prompt, wkv_p[None], shift_prompt, conv_prompt,
            k_sample, v_sample, wkv_s[None], shift_sample, conv_sample)
```
